```python
import math, functools
import jax, jax.numpy as jnp
from jax import lax
import numpy as np

D_MODEL = 2048
BATCH = 2
SEQ = 4096
DEPTH = 2
DEC_BATCH = 128
DEC_SEQ = 1
PAST_LEN = 2048
PAGE_SIZE = 128

N_META = 16
D_MIX = D_MODEL
SSD_WIDTH = D_MIX // 4
SSD_HEAD_DIM = 64
SSD_HEADS = SSD_WIDTH // SSD_HEAD_DIM
SSD_GROUPS = 2
SSD_STATE = 128
SSD_CONV = 4
SSD_CHUNK = 128
SSD_CONV_CH = SSD_WIDTH + 2 * SSD_GROUPS * SSD_STATE
S5_WIDTH = D_MIX // 4
S5_GROUP_CH = 16
S5_GROUPS = S5_WIDTH // S5_GROUP_CH
S5_STATE = 64
ATT_WIDTH = D_MIX - SSD_WIDTH - S5_WIDTH
ATT_HEAD_DIM = 128
ATT_HEADS = ATT_WIDTH // ATT_HEAD_DIM
ATT_BLOCK = 128
FORGET_BIAS = 7.0
D_FF = 5632
NORM_EPS = 1e-6
IN_SIZES = (SSD_WIDTH, SSD_CONV_CH, SSD_HEADS, S5_WIDTH, ATT_WIDTH, ATT_WIDTH, ATT_WIDTH, ATT_HEADS)
IN_WIDTH = sum(IN_SIZES)
IN_SPLITS = tuple(int(s) for s in np.cumsum(IN_SIZES)[:-1])

kernel_name = 'hymba_ssd_s5_fox_macaron_step'


def rms_norm(x, g):
    xf = x.astype(jnp.float32)
    y = xf * lax.rsqrt(jnp.mean(xf * xf, axis=-1, keepdims=True) + NORM_EPS)
    return (y * g.astype(jnp.float32)).astype(x.dtype)


def swiglu(x, w_gate, w_up, w_down):
    return (jax.nn.silu(x @ w_gate) * (x @ w_up)) @ w_down


def causal_conv(xbc, buf, w, b):
    ch = xbc.shape[-1]
    full = jnp.concatenate([buf.astype(xbc.dtype), xbc], axis=1)
    out = lax.conv_general_dilated(full, w.astype(xbc.dtype)[:, None, :], window_strides=(1,), padding='VALID',
                                   dimension_numbers=('NWC', 'WIO', 'NWC'), feature_group_count=ch)
    return jax.nn.silu(out + b.astype(xbc.dtype)), full[:, full.shape[1] - (SSD_CONV - 1):]


def ssd_scan(x, dt, a, bm, cm, h0, lpad):
    bsz, l, nh, hp = x.shape
    ng, ns = bm.shape[-2:]
    nr = nh // ng
    q = SSD_CHUNK
    rpad = (-(lpad + l)) % q

    def pad(t):
        return jnp.pad(t, [(0, 0), (lpad, rpad)] + [(0, 0)] * (t.ndim - 2))

    nc = (lpad + l + rpad) // q
    x = pad(x).reshape(bsz, nc, q, ng, nr, hp)
    dt = pad(dt).reshape(bsz, nc, q, ng, nr)
    bm = pad(bm).reshape(bsz, nc, q, ng, ns)
    cm = pad(cm).reshape(bsz, nc, q, ng, ns)
    a_cs = jnp.cumsum(dt * a.reshape(ng, nr), axis=2)
    seg = a_cs[:, :, :, None] - a_cs[:, :, None, :]
    causal = jnp.tril(jnp.ones((q, q), dtype=bool))[None, None, :, :, None, None]
    decay = jnp.exp(jnp.where(causal, seg, -jnp.inf))
    xdt = x * dt[..., None]
    cb = jnp.einsum('bctgn,bcsgn->bctsg', cm, bm)
    y_diag = jnp.einsum('bctsgr,bcsgrp->bctgrp', cb[..., None] * decay, xdt)
    to_end = jnp.exp(a_cs[:, :, -1:] - a_cs)
    local = jnp.einsum('bcsgn,bcsgrp->bcgrpn', bm, xdt * to_end[..., None])
    chunk_decay = jnp.exp(a_cs[:, :, -1])

    def step(h, inp):
        loc, dec = inp
        return h * dec[..., None, None] + loc, h

    h_last, h_start = lax.scan(step, h0.reshape(bsz, ng, nr, hp, ns),
                               (jnp.moveaxis(local, 1, 0), jnp.moveaxis(chunk_decay, 1, 0)))
    h_start = jnp.moveaxis(h_start, 0, 1)
    y_off = jnp.einsum('bctgn,bcgrpn->bctgrp', cm, h_start) * jnp.exp(a_cs)[..., None]
    y = (y_diag + y_off).reshape(bsz, nc * q, nh, hp)[:, lpad:lpad + l]
    return y, h_last.reshape(bsz, nh, hp, ns)


def _cmul_combine(e1, e2):
    a1r, a1i, b1r, b1i = e1
    a2r, a2i, b2r, b2i = e2
    return (a2r * a1r - a2i * a1i, a2r * a1i + a2i * a1r,
            a2r * b1r - a2i * b1i + b2r, a2r * b1i + a2i * b1r + b2i)


def s5_scan(u, a_re, a_im, log_step, b_re, b_im, c_re, c_im, d_skip, s0_re, s0_im):
    f32 = jnp.float32
    a_re, a_im, b_re, b_im, c_re, c_im, d_skip = (t.astype(f32) for t in (a_re, a_im, b_re, b_im, c_re, c_im, d_skip))
    step = jnp.exp(log_step.astype(f32))[:, None]
    mag = jnp.exp(a_re * step)
    lam_re = mag * jnp.cos(a_im * step)
    lam_im = mag * jnp.sin(a_im * step)
    den = a_re * a_re + a_im * a_im
    zr = ((lam_re - 1.0) * a_re + lam_im * a_im) / den
    zi = (lam_im * a_re - (lam_re - 1.0) * a_im) / den
    bb_re = zr[..., None] * b_re - zi[..., None] * b_im
    bb_im = zr[..., None] * b_im + zi[..., None] * b_re
    bu_re = jnp.einsum('gnc,blgc->blgn', bb_re, u)
    bu_im = jnp.einsum('gnc,blgc->blgn', bb_im, u)
    lr = jnp.broadcast_to(lam_re, bu_re.shape)
    li = jnp.broadcast_to(lam_im, bu_re.shape)
    ar, ai, sr, si = lax.associative_scan(_cmul_combine, (lr, li, bu_re, bu_im), axis=1)
    s0_re = s0_re.astype(f32)[:, None]
    s0_im = s0_im.astype(f32)[:, None]
    s_re = sr + ar * s0_re - ai * s0_im
    s_im = si + ar * s0_im + ai * s0_re
    y = jnp.einsum('gcn,blgn->blgc', c_re, s_re) - jnp.einsum('gcn,blgn->blgc', c_im, s_im) + d_skip * u
    return y, s_re[:, -1], s_im[:, -1]


def fox_prompt(q, k, v, logf):
    b, l, nh, hd = q.shape
    scale = hd ** -0.5
    c = jnp.cumsum(logf, axis=1)
    nb = -(-l // ATT_BLOCK)
    lq = nb * ATT_BLOCK
    qb = jnp.pad(q, ((0, 0), (0, lq - l), (0, 0), (0, 0))).reshape(b, nb, ATT_BLOCK, nh, hd).swapaxes(0, 1)
    cq = jnp.pad(c, ((0, 0), (0, lq - l), (0, 0))).reshape(b, nb, ATT_BLOCK, nh).swapaxes(0, 1)
    qpos = jnp.arange(lq).reshape(nb, ATT_BLOCK)
    kpos = jnp.arange(l)
    ck = c.swapaxes(1, 2)

    def block(args):
        qi, ci, pi = args
        s = jnp.einsum('bqhd,bkhd->bhqk', qi, k) * scale + ci.swapaxes(1, 2)[..., None] - ck[:, :, None, :]
        s = jnp.where(kpos[None, :] <= pi[:, None], s, -jnp.inf)
        p = jax.nn.softmax(s, axis=-1)
        return jnp.einsum('bhqk,bkhd->bqhd', p, v)

    o = lax.map(block, (qb, cq, qpos))
    return o.swapaxes(0, 1).reshape(b, lq, nh, hd)[:, :l]


def fox_sample(q, k, v, logf, k_past, v_past, logf_past):
    f32 = jnp.float32
    hd = q.shape[-1]
    t = q.shape[1]
    scale = hd ** -0.5
    lp = logf_past.astype(f32)
    suffix = lax.cumsum(lp, axis=1, reverse=True) - lp
    c_new = jnp.cumsum(logf, axis=1)
    cq = c_new.swapaxes(1, 2)[..., None]
    s_past = jnp.einsum('bqhd,bkhd->bhqk', q, k_past.astype(f32)) * scale + cq + suffix.swapaxes(1, 2)[:, :, None, :]
    s_new = jnp.einsum('bqhd,bkhd->bhqk', q, k) * scale + cq - c_new.swapaxes(1, 2)[:, :, None, :]
    s_new = jnp.where(jnp.tril(jnp.ones((t, t), dtype=bool)), s_new, -jnp.inf)
    p = jax.nn.softmax(jnp.concatenate([s_past, s_new], axis=-1), axis=-1)
    n_past = k_past.shape[1]
    return (jnp.einsum('bhqk,bkhd->bqhd', p[..., :n_past], v_past.astype(f32))
            + jnp.einsum('bhqk,bkhd->bqhd', p[..., n_past:], v))


def trunk_layer(h, lw, ssd_h0, conv_buf, s5_re0, s5_im0, ssd_lpad, attend):
    f32 = jnp.float32
    b, l, _ = h.shape
    h = h + 0.5 * swiglu(rms_norm(h, lw['ffn1_norm']), lw['ffn1_gate'], lw['ffn1_up'], lw['ffn1_down'])
    hn = rms_norm(h, lw['mix_norm'])
    proj = hn @ lw['w_in']
    z, xbc, dt_raw, u, q, k, v, f_raw = jnp.split(proj, IN_SPLITS, axis=-1)
    xbc_c, conv_new = causal_conv(xbc, conv_buf, lw['ssd_conv_w'], lw['ssd_conv_b'])
    xs, bm, cm = jnp.split(xbc_c.astype(f32), [SSD_WIDTH, SSD_WIDTH + SSD_GROUPS * SSD_STATE], axis=-1)
    xs = xs.reshape(b, l, SSD_HEADS, SSD_HEAD_DIM)
    bm = bm.reshape(b, l, SSD_GROUPS, SSD_STATE)
    cm = cm.reshape(b, l, SSD_GROUPS, SSD_STATE)
    dt = jax.nn.softplus(dt_raw.astype(f32) + lw['ssd_dt_bias'].astype(f32))
    a = -jnp.exp(lw['ssd_A_log'].astype(f32))
    y, ssd_h = ssd_scan(xs, dt, a, bm, cm, ssd_h0.astype(f32), ssd_lpad)
    y = y + lw['ssd_D'].astype(f32)[:, None] * xs
    yg = (y.reshape(b, l, SSD_WIDTH) * jax.nn.silu(z.astype(f32))).reshape(b, l, SSD_GROUPS, SSD_WIDTH // SSD_GROUPS)
    yg = yg * lax.rsqrt(jnp.mean(yg * yg, axis=-1, keepdims=True) + NORM_EPS)
    y_ssd = yg.reshape(b, l, SSD_WIDTH) * lw['ssd_norm'].astype(f32)
    uf = u.astype(f32).reshape(b, l, S5_GROUPS, S5_GROUP_CH)
    ys5, s5_re, s5_im = s5_scan(uf, lw['s5_A_re'], lw['s5_A_im'], lw['s5_log_step'], lw['s5_B_re'], lw['s5_B_im'],
                                lw['s5_C_re'], lw['s5_C_im'], lw['s5_D'], s5_re0, s5_im0)
    g5 = jax.nn.gelu(ys5.reshape(b, l, S5_WIDTH))
    y_s5 = g5 * jax.nn.sigmoid(g5 @ lw['s5_w_glu'].astype(f32) + lw['s5_b_glu'].astype(f32))
    qh = rms_norm(q.reshape(b, l, ATT_HEADS, ATT_HEAD_DIM), lw['att_q_norm']).astype(f32)
    kh = rms_norm(k.reshape(b, l, ATT_HEADS, ATT_HEAD_DIM), lw['att_k_norm'])
    vh = v.reshape(b, l, ATT_HEADS, ATT_HEAD_DIM)
    logf = jax.nn.log_sigmoid(f_raw.astype(f32) + lw['att_f_bias'].astype(f32))
    y_att = attend(qh, kh.astype(f32), vh.astype(f32), logf).reshape(b, l, ATT_WIDTH)
    mixed = jnp.concatenate([y_ssd, y_s5, y_att], axis=-1).astype(h.dtype) @ lw['w_out']
    h = h + mixed
    h = h + 0.5 * swiglu(rms_norm(h, lw['ffn2_norm']), lw['ffn2_gate'], lw['ffn2_up'], lw['ffn2_down'])
    return h, (ssd_h, conv_new, s5_re, s5_im, kh, vh, logf)


def setup_inputs(seed: int = 0) -> dict:
    key = jax.random.key(seed)
    keys = iter(jax.random.split(key, 64))
    f32 = jnp.float32

    def nrm(shape, scale):
        return jax.random.normal(next(keys), shape, f32) * scale

    def gain(shape):
        return 1.0 + nrm(shape, 0.02)

    def unif(shape, lo, hi):
        return jax.random.uniform(next(keys), shape, f32, lo, hi)

    n_pages = PAST_LEN // PAGE_SIZE
    n_used = DEC_BATCH * n_pages
    n_phys = n_used + n_used // 4
    inp = {}
    inp['x_prompt'] = nrm((BATCH, SEQ, D_MODEL), 1.0)
    inp['x_sample'] = nrm((DEC_BATCH, DEC_SEQ, D_MODEL), 1.0)
    inp['cache_k'] = nrm((DEPTH, n_phys, PAGE_SIZE, ATT_HEADS, ATT_HEAD_DIM), 1.0)
    inp['cache_v'] = nrm((DEPTH, n_phys, PAGE_SIZE, ATT_HEADS, ATT_HEAD_DIM), 1.0)
    inp['cache_logf'] = jax.nn.log_sigmoid(FORGET_BIAS + nrm((DEPTH, n_phys, PAGE_SIZE, ATT_HEADS), 1.0))
    inp['state_ssd'] = nrm((DEPTH, DEC_BATCH, SSD_HEADS, SSD_HEAD_DIM, SSD_STATE), 0.1)
    inp['state_ssd_conv'] = nrm((DEPTH, DEC_BATCH, SSD_CONV - 1, SSD_CONV_CH), 1.0)
    inp['state_s5_re'] = nrm((DEPTH, DEC_BATCH, S5_GROUPS, S5_STATE), 0.3)
    inp['state_s5_im'] = nrm((DEPTH, DEC_BATCH, S5_GROUPS, S5_STATE), 0.3)
    perm = jax.random.permutation(next(keys), n_phys)
    inp['page_table'] = perm[:n_used].reshape(DEC_BATCH, n_pages).astype(jnp.int32)
    inp['meta_tokens'] = nrm((N_META, D_MODEL), 1.0)
    inp['ffn1_norm'] = gain((DEPTH, D_MODEL))
    inp['ffn1_gate'] = nrm((DEPTH, D_MODEL, D_FF), D_MODEL ** -0.5)
    inp['ffn1_up'] = nrm((DEPTH, D_MODEL, D_FF), D_MODEL ** -0.5)
    inp['ffn1_down'] = nrm((DEPTH, D_FF, D_MODEL), D_FF ** -0.5)
    inp['mix_norm'] = gain((DEPTH, D_MODEL))
    inp['w_in'] = nrm((DEPTH, D_MODEL, IN_WIDTH), D_MODEL ** -0.5)
    inp['ssd_conv_w'] = nrm((DEPTH, SSD_CONV, SSD_CONV_CH), SSD_CONV ** -0.5)
    inp['ssd_conv_b'] = nrm((DEPTH, SSD_CONV_CH), 0.02)
    dt0 = jnp.exp(unif((DEPTH, SSD_HEADS), math.log(1e-3), math.log(1e-1)))
    inp['ssd_dt_bias'] = dt0 + jnp.log(-jnp.expm1(-dt0))
    inp['ssd_A_log'] = jnp.log(unif((DEPTH, SSD_HEADS), 1.0, 16.0))
    inp['ssd_D'] = 1.0 + nrm((DEPTH, SSD_HEADS), 0.1)
    inp['ssd_norm'] = gain((DEPTH, SSD_WIDTH))
    inp['s5_A_re'] = -0.5 + nrm((DEPTH, S5_GROUPS, S5_STATE), 0.01)
    inp['s5_A_im'] = jnp.pi * jnp.arange(S5_STATE, dtype=f32) + nrm((DEPTH, S5_GROUPS, S5_STATE), 0.01)
    inp['s5_log_step'] = unif((DEPTH, S5_GROUPS), math.log(1e-3), math.log(1e-1))
    inp['s5_B_re'] = nrm((DEPTH, S5_GROUPS, S5_STATE, S5_GROUP_CH), (2 * S5_GROUP_CH) ** -0.5)
    inp['s5_B_im'] = nrm((DEPTH, S5_GROUPS, S5_STATE, S5_GROUP_CH), (2 * S5_GROUP_CH) ** -0.5)
    inp['s5_C_re'] = nrm((DEPTH, S5_GROUPS, S5_GROUP_CH, S5_STATE), (2 * S5_STATE) ** -0.5)
    inp['s5_C_im'] = nrm((DEPTH, S5_GROUPS, S5_GROUP_CH, S5_STATE), (2 * S5_STATE) ** -0.5)
    inp['s5_D'] = nrm((DEPTH, S5_GROUPS, S5_GROUP_CH), 0.5)
    inp['s5_w_glu'] = nrm((DEPTH, S5_WIDTH, S5_WIDTH), S5_WIDTH ** -0.5)
    inp['s5_b_glu'] = nrm((DEPTH, S5_WIDTH), 0.02)
    inp['att_q_norm'] = gain((DEPTH, ATT_HEAD_DIM))
    inp['att_k_norm'] = gain((DEPTH, ATT_HEAD_DIM))
    inp['att_f_bias'] = FORGET_BIAS + nrm((DEPTH, ATT_HEADS), 0.5)
    inp['w_out'] = nrm((DEPTH, D_MIX, D_MODEL), D_MIX ** -0.5)
    inp['ffn2_norm'] = gain((DEPTH, D_MODEL))
    inp['ffn2_gate'] = nrm((DEPTH, D_MODEL, D_FF), D_MODEL ** -0.5)
    inp['ffn2_up'] = nrm((DEPTH, D_MODEL, D_FF), D_MODEL ** -0.5)
    inp['ffn2_down'] = nrm((DEPTH, D_FF, D_MODEL), D_FF ** -0.5)
    return inp


def reference(x_prompt, x_sample, cache_k, cache_v, cache_logf, state_ssd, state_ssd_conv, state_s5_re, state_s5_im,
              page_table, meta_tokens, ffn1_norm, ffn1_gate, ffn1_up, ffn1_down, mix_norm, w_in, ssd_conv_w,
              ssd_conv_b, ssd_dt_bias, ssd_A_log, ssd_D, ssd_norm, s5_A_re, s5_A_im, s5_log_step, s5_B_re, s5_B_im,
              s5_C_re, s5_C_im, s5_D, s5_w_glu, s5_b_glu, att_q_norm, att_k_norm, att_f_bias, w_out, ffn2_norm,
              ffn2_gate, ffn2_up, ffn2_down):
    f32 = jnp.float32
    bp = x_prompt.shape[0]
    bs = x_sample.shape[0]
    past_len = page_table.shape[1] * PAGE_SIZE
    hp = jnp.concatenate([jnp.broadcast_to(meta_tokens.astype(x_prompt.dtype), (bp, N_META, D_MODEL)), x_prompt], axis=1)
    hs = x_sample
    zero_ssd = jnp.zeros((bp, SSD_HEADS, SSD_HEAD_DIM, SSD_STATE), f32)
    zero_conv = jnp.zeros((bp, SSD_CONV - 1, SSD_CONV_CH), x_prompt.dtype)
    zero_s5 = jnp.zeros((bp, S5_GROUPS, S5_STATE), f32)
    lpad_prompt = (0 - N_META) % SSD_CHUNK
    lpad_sample = (past_len - N_META) % SSD_CHUNK
    outs_p = []
    outs_s = []
    for i in range(DEPTH):
        lw = dict(ffn1_norm=ffn1_norm[i], ffn1_gate=ffn1_gate[i], ffn1_up=ffn1_up[i], ffn1_down=ffn1_down[i],
                  mix_norm=mix_norm[i], w_in=w_in[i], ssd_conv_w=ssd_conv_w[i], ssd_conv_b=ssd_conv_b[i],
                  ssd_dt_bias=ssd_dt_bias[i], ssd_A_log=ssd_A_log[i], ssd_D=ssd_D[i], ssd_norm=ssd_norm[i],
                  s5_A_re=s5_A_re[i], s5_A_im=s5_A_im[i], s5_log_step=s5_log_step[i], s5_B_re=s5_B_re[i],
                  s5_B_im=s5_B_im[i], s5_C_re=s5_C_re[i], s5_C_im=s5_C_im[i], s5_D=s5_D[i], s5_w_glu=s5_w_glu[i],
                  s5_b_glu=s5_b_glu[i], att_q_norm=att_q_norm[i], att_k_norm=att_k_norm[i], att_f_bias=att_f_bias[i],
                  w_out=w_out[i], ffn2_norm=ffn2_norm[i], ffn2_gate=ffn2_gate[i], ffn2_up=ffn2_up[i],
                  ffn2_down=ffn2_down[i])
        hp, st_p = trunk_layer(hp, lw, zero_ssd, zero_conv, zero_s5, zero_s5, lpad_prompt, fox_prompt)
        k_past = cache_k[i][page_table].reshape(bs, past_len, ATT_HEADS, ATT_HEAD_DIM)
        v_past = cache_v[i][page_table].reshape(bs, past_len, ATT_HEADS, ATT_HEAD_DIM)
        lf_past = cache_logf[i][page_table].reshape(bs, past_len, ATT_HEADS)
        attend_s = functools.partial(fox_sample, k_past=k_past, v_past=v_past, logf_past=lf_past)
        hs, st_s = trunk_layer(hs, lw, state_ssd[i], state_ssd_conv[i], state_s5_re[i], state_s5_im[i],
                               lpad_sample, attend_s)
        outs_p.append(st_p)
        outs_s.append(st_s)

    def stack(outs, j):
        return jnp.stack([o[j] for o in outs], axis=0)

    y_prompt = hp[:, N_META:]
    y_sample = hs
    ssd_prompt, conv_prompt, s5_re_prompt, s5_im_prompt = stack(outs_p, 0), stack(outs_p, 1), stack(outs_p, 2), stack(outs_p, 3)
    k_prompt, v_prompt, logf_prompt = stack(outs_p, 4), stack(outs_p, 5), stack(outs_p, 6)
    ssd_sample, conv_sample, s5_re_sample, s5_im_sample = stack(outs_s, 0), stack(outs_s, 1), stack(outs_s, 2), stack(outs_s, 3)
    k_sample, v_sample, logf_sample = stack(outs_s, 4), stack(outs_s, 5), stack(outs_s, 6)
    return (y_prompt, y_sample, k_prompt, v_prompt, logf_prompt, k_sample, v_sample, logf_sample,
            ssd_prompt, ssd_sample, conv_prompt, conv_sample, s5_re_prompt, s5_re_sample, s5_im_prompt, s5_im_sample)
```

```python
import functools
import math

import jax
import jax.numpy as jnp
from jax import lax
from jax.experimental import pallas as pl
from jax.experimental.pallas import tpu as pltpu

F32 = jnp.float32
BF16 = jnp.bfloat16
HIGHEST = lax.Precision.HIGHEST

NORM_EPS = 1e-6
N_META = 16
PAGE = 128
CHUNK = 128
SSD_HEADS, SSD_P, SSD_N, SSD_G = 8, 64, 128, 2
SSD_W = SSD_HEADS * SSD_P
S5_G, S5_C, S5_N = 32, 16, 64
S5_W = S5_G * S5_C
S5_S = S5_G * S5_N
ATT_H, ATT_D = 8, 128
ATT_W = ATT_H * ATT_D
PROJ_TN = 512
COL_Z, COL_X, COL_BC, COL_U, COL_Q, COL_K, COL_V, COL_DTF = 0, 512, 1024, 1536, 2048, 3072, 4096, 5120
PROJ_W = 5632
VMEM_LIMIT = 56 * 1024 * 1024


def _cparams(sem):
    return pltpu.CompilerParams(dimension_semantics=sem, vmem_limit_bytes=VMEM_LIMIT)


def _rms(x, g):
    return x * lax.rsqrt(jnp.mean(x * x, axis=-1, keepdims=True) + NORM_EPS) * g


def _silu(x):
    return x * jax.nn.sigmoid(x)


def _softplus(x):
    return jnp.maximum(x, 0.0) + jnp.log1p(jnp.exp(-jnp.abs(x)))


def _ffn_kernel(x_ref, g_ref, wg_ref, wu_ref, wd_ref, o_ref, xn_ref):
    j = pl.program_id(1)

    @pl.when(j == 0)
    def _():
        x = x_ref[...]
        xn_ref[...] = _rms(x, g_ref[...]).astype(BF16)
        o_ref[...] = x

    xn = xn_ref[...]
    gate = jnp.dot(xn, wg_ref[...], preferred_element_type=F32)
    up = jnp.dot(xn, wu_ref[...], preferred_element_type=F32)
    act = (0.5 * _silu(gate) * up).astype(BF16)
    o_ref[...] += jnp.dot(act, wd_ref[...], preferred_element_type=F32)


def _ffn(h, g, wg, wu, wd, tm=512, tf=512):
    r, d = h.shape
    f = wg.shape[1]
    return pl.pallas_call(
        _ffn_kernel,
        grid=(r // tm, f // tf),
        in_specs=[pl.BlockSpec((tm, d), lambda i, j: (i, 0)),
                  pl.BlockSpec((1, d), lambda i, j: (0, 0)),
                  pl.BlockSpec((d, tf), lambda i, j: (0, j)),
                  pl.BlockSpec((d, tf), lambda i, j: (0, j)),
                  pl.BlockSpec((tf, d), lambda i, j: (j, 0))],
        out_specs=pl.BlockSpec((tm, d), lambda i, j: (i, 0)),
        out_shape=jax.ShapeDtypeStruct((r, d), F32),
        scratch_shapes=[pltpu.VMEM((tm, d), BF16)],
        compiler_params=_cparams(("parallel", "arbitrary")),
        name="ffn",
    )(h, g, wg, wu, wd)


def _inproj_kernel(x_ref, g_ref, w_ref, gain_ref, bias_ref, o_ref, xn_ref):
    j = pl.program_id(1)

    @pl.when(j == 0)
    def _():
        xn_ref[...] = _rms(x_ref[...], g_ref[...]).astype(BF16)

    acc = jnp.dot(xn_ref[...], w_ref[...], preferred_element_type=F32)
    jq, jv, jd = COL_Q // PROJ_TN, COL_V // PROJ_TN, COL_DTF // PROJ_TN
    is_qk = jnp.logical_and(j >= jq, j < jv)

    @pl.when(jnp.logical_and(jnp.logical_not(is_qk), j != jd))
    def _():
        o_ref[...] = acc

    @pl.when(is_qk)
    def _():
        for hh in range(PROJ_TN // ATT_D):
            sl = slice(hh * ATT_D, (hh + 1) * ATT_D)
            o_ref[:, sl] = _rms(acc[:, sl], gain_ref[:, sl])

    @pl.when(j == jd)
    def _():
        a = acc[:, :128] + bias_ref[:, :128]
        lane = lax.broadcasted_iota(jnp.int32, a.shape, 1)
        is_dt = lane < SSD_HEADS
        sp = _softplus(jnp.where(is_dt, a, -a))
        o_ref[:, :128] = jnp.where(is_dt, sp, -sp)
        o_ref[:, 128:] = jnp.zeros((acc.shape[0], PROJ_TN - 128), F32)


def _inproj(h, g, w, gains, bias, tm=512):
    r, d = h.shape
    n = w.shape[1]
    tn = PROJ_TN
    return pl.pallas_call(
        _inproj_kernel,
        grid=(r // tm, n // tn),
        in_specs=[pl.BlockSpec((tm, d), lambda i, j: (i, 0)),
                  pl.BlockSpec((1, d), lambda i, j: (0, 0)),
                  pl.BlockSpec((d, tn), lambda i, j: (0, j)),
                  pl.BlockSpec((None, 1, tn), lambda i, j: (j, 0, 0)),
                  pl.BlockSpec((1, tn), lambda i, j: (0, 0))],
        out_specs=pl.BlockSpec((tm, tn), lambda i, j: (i, j)),
        out_shape=jax.ShapeDtypeStruct((r, n), F32),
        scratch_shapes=[pltpu.VMEM((tm, d), BF16)],
        compiler_params=_cparams(("parallel", "arbitrary")),
        name="inproj",
    )(h, g, w, gains, bias)


def _ssd_gate_norm(y, z, gn):
    yz = y * _silu(z)
    gw = SSD_W // SSD_G
    outs = []
    for g in range(SSD_G):
        blk = yz[:, g * gw:(g + 1) * gw]
        outs.append(blk * lax.rsqrt(jnp.mean(blk * blk, axis=-1, keepdims=True) + NORM_EPS)
                    * gn[:, g * gw:(g + 1) * gw])
    return outs


def _ssd_prompt_kernel(z_ref, xx_ref, bc_ref, dtf_ref, cw_ref, cb_ref, alog_ref, dexp_ref, gn_ref,
                       y_ref, catt_ref, st_ref, cst_ref, xbuf, hst, ccarry, ybuf, *, l_real, n_chunks):
    c = pl.program_id(1)
    q = CHUNK

    @pl.when(c == 0)
    def _():
        xbuf[0:8, :] = jnp.zeros((8, xbuf.shape[1]), F32)
        hst[...] = jnp.zeros(hst.shape, F32)
        ccarry[...] = jnp.zeros(ccarry.shape, F32)

    xbuf[8:8 + q, 0:SSD_W] = xx_ref[...]
    xbuf[8:8 + q, SSD_W:] = bc_ref[...]
    conv = cb_ref[...] + cw_ref[0:1, :] * xbuf[5:5 + q, :]
    for k in range(1, 4):
        conv = conv + cw_ref[k:k + 1, :] * xbuf[5 + k:5 + k + q, :]
    xc = _silu(conv)

    @pl.when(c == n_chunks - 1)
    def _():
        r0 = 8 + l_real - q * (n_chunks - 1) - 3
        cst_ref[...] = xbuf[r0:r0 + 3, :]

    xbuf[0:8, :] = xbuf[q:q + 8, :]

    row = lax.broadcasted_iota(jnp.int32, (q, q), 0)
    col = lax.broadcasted_iota(jnp.int32, (q, q), 1)
    lane1 = lax.broadcasted_iota(jnp.int32, (1, q), 1)
    valid = (c * q + row) < l_real
    dtm = jnp.where(valid, dtf_ref[...], 0.0)
    a_row = -jnp.exp(alog_ref[...])
    scale_row = jnp.where(lane1 < SSD_HEADS, a_row, jnp.where(lane1 < 2 * SSD_HEADS, 1.0, 0.0))
    causal = row >= col
    cs = jnp.dot(causal.astype(F32), dtm * scale_row, precision=HIGHEST, preferred_element_type=F32)
    cs_t = cs.T
    catt_ref[...] = (cs + ccarry[0:1, :]).T[SSD_HEADS:2 * SSD_HEADS, :]
    ccarry[0:1, :] = ccarry[0:1, :] + cs[q - 1:q, :]

    hpg = SSD_HEADS // SSD_G
    for g in range(SSD_G):
        bg = xc[:, SSD_W + g * SSD_N:SSD_W + (g + 1) * SSD_N]
        cg = xc[:, SSD_W + (SSD_G + g) * SSD_N:SSD_W + (SSD_G + g + 1) * SSD_N].astype(BF16)
        bg_t = bg.T.astype(BF16)
        cb = jnp.dot(cg, bg_t, preferred_element_type=F32)
        for hh in range(hpg):
            h = g * hpg + hh
            xh = xc[:, h * SSD_P:(h + 1) * SSD_P]
            acs_col = cs[:, h:h + 1]
            seg = acs_col - cs_t[h:h + 1, :]
            decay = jnp.exp(jnp.where(causal, seg, -jnp.inf))
            xdt = xh * dtm[:, h:h + 1]
            y = jnp.dot((cb * decay).astype(BF16), xdt.astype(BF16), preferred_element_type=F32)
            h_t = hst[h]
            y = y + jnp.dot(cg, h_t.astype(BF16), preferred_element_type=F32) * jnp.exp(acs_col)
            a_last = cs[q - 1:q, h:h + 1]
            xw = (xdt * jnp.exp(a_last - acs_col)).astype(BF16)
            hst[h] = h_t * jnp.exp(a_last) + jnp.dot(bg_t, xw, preferred_element_type=F32)
            ybuf[:, h * SSD_P:(h + 1) * SSD_P] = y + dexp_ref[:, h * SSD_P:(h + 1) * SSD_P] * xh

    outs = _ssd_gate_norm(ybuf[...], z_ref[...], gn_ref[...])
    gw = SSD_W // SSD_G
    for g in range(SSD_G):
        y_ref[:, g * gw:(g + 1) * gw] = outs[g].astype(BF16)

    @pl.when(c == n_chunks - 1)
    def _():
        st_ref[...] = hst[...]


def _ssd_prompt(proj, cw, cb, alog_row, dexp, gn, *, bp, lp, l_real):
    nc = lp // CHUNK
    cw_b = SSD_W // 512
    kern = functools.partial(_ssd_prompt_kernel, l_real=l_real, n_chunks=nc)
    rowblk = lambda b, c: b * nc + c
    return pl.pallas_call(
        kern,
        grid=(bp, nc),
        in_specs=[pl.BlockSpec((CHUNK, 512), lambda b, c: (rowblk(b, c), COL_Z // 512)),
                  pl.BlockSpec((CHUNK, 512), lambda b, c: (rowblk(b, c), COL_X // 512)),
                  pl.BlockSpec((CHUNK, 512), lambda b, c: (rowblk(b, c), COL_BC // 512)),
                  pl.BlockSpec((CHUNK, 128), lambda b, c: (rowblk(b, c), COL_DTF // 128)),
                  pl.BlockSpec((4, 1024), lambda b, c: (0, 0)),
                  pl.BlockSpec((1, 1024), lambda b, c: (0, 0)),
                  pl.BlockSpec((1, 128), lambda b, c: (0, 0)),
                  pl.BlockSpec((1, 512), lambda b, c: (0, 0)),
                  pl.BlockSpec((1, 512), lambda b, c: (0, 0))],
        out_specs=[pl.BlockSpec((CHUNK, SSD_W), lambda b, c: (rowblk(b, c), 0)),
                   pl.BlockSpec((None, SSD_HEADS, CHUNK), lambda b, c: (b, 0, c)),
                   pl.BlockSpec((None, SSD_HEADS, SSD_N, SSD_P), lambda b, c: (b, 0, 0, 0)),
                   pl.BlockSpec((None, 3, 1024), lambda b, c: (b, 0, 0))],
        out_shape=[jax.ShapeDtypeStruct((bp * lp, SSD_W), BF16),
                   jax.ShapeDtypeStruct((bp, ATT_H, lp), F32),
                   jax.ShapeDtypeStruct((bp, SSD_HEADS, SSD_N, SSD_P), F32),
                   jax.ShapeDtypeStruct((bp, 3, 1024), F32)],
        scratch_shapes=[pltpu.VMEM((CHUNK + 8, 1024), F32),
                        pltpu.VMEM((SSD_HEADS, SSD_N, SSD_P), F32),
                        pltpu.VMEM((8, 128), F32),
                        pltpu.VMEM((CHUNK, SSD_W), F32)],
        compiler_params=_cparams(("parallel", "arbitrary")),
        name="ssd_prompt",
    )(proj, proj, proj, proj, cw, cb, alog_row, dexp, gn)


SSD_SB = 16


def _ssd_sample_kernel(z_ref, xx_ref, bc_ref, dtf_ref, cst_ref, st_ref, cw_ref, cb_ref, aexp_ref, dexp_ref,
                       gn_ref, expand_ref, y_ref, cst_out, st_out, dec_t, xdt_t, bm_s, cm_s, y_t, *, n_steps):
    i = pl.program_id(0)
    bs = z_ref.shape[0]

    @pl.when(i == 0)
    def _():
        w = cw_ref[...]
        for half, ref in ((0, xx_ref), (1, bc_ref)):
            sl = slice(half * 512, (half + 1) * 512)
            xn = ref[...]
            conv = (cb_ref[:, sl] + w[0:1, sl] * cst_ref[0, :, sl] + w[1:2, sl] * cst_ref[1, :, sl]
                    + w[2:3, sl] * cst_ref[2, :, sl] + w[3:4, sl] * xn)
            xcv = _silu(conv)
            cst_out[0, :, sl] = cst_ref[1, :, sl]
            cst_out[1, :, sl] = cst_ref[2, :, sl]
            cst_out[2, :, sl] = xn
            if half == 0:
                dt_e = jnp.dot(dtf_ref[...], expand_ref[...], precision=HIGHEST, preferred_element_type=F32)
                dec_t[...] = jnp.exp(dt_e * (-jnp.exp(aexp_ref[...]))).T
                xdt_t[...] = (xcv * dt_e).T
                y_t[...] = (dexp_ref[...] * xcv).T
            else:
                bm_s[...] = xcv[:, 0:SSD_G * SSD_N]
                cm_s[...] = xcv[:, SSD_G * SSD_N:]

    shift = (bs - i * SSD_SB) % bs
    dec_blk = pltpu.roll(dec_t[...], shift, 1)
    xdt_blk = pltpu.roll(xdt_t[...], shift, 1)
    lane = lax.broadcasted_iota(jnp.int32, (1, bs), 1)
    hpg = SSD_HEADS // SSD_G
    yacc = [jnp.zeros((SSD_P, bs), F32) for _ in range(SSD_HEADS)]
    for bb in range(SSD_SB):
        onehot = (lane == bb).astype(F32)
        brow = bm_s[pl.ds(i * SSD_SB + bb, 1), :]
        crow = cm_s[pl.ds(i * SSD_SB + bb, 1), :]
        for h in range(SSD_HEADS):
            g = h // hpg
            rs = slice(h * SSD_P, (h + 1) * SSD_P)
            hn = (st_ref[bb, h] * dec_blk[rs, bb:bb + 1]
                  + xdt_blk[rs, bb:bb + 1] * brow[:, g * SSD_N:(g + 1) * SSD_N])
            st_out[bb, h] = hn
            ycol = jnp.sum(hn * crow[:, g * SSD_N:(g + 1) * SSD_N], axis=-1, keepdims=True)
            yacc[h] = yacc[h] + ycol * onehot
    for h in range(SSD_HEADS):
        rs = slice(h * SSD_P, (h + 1) * SSD_P)
        y_t[rs, :] = y_t[rs, :] + pltpu.roll(yacc[h], i * SSD_SB, 1)

    @pl.when(i == n_steps - 1)
    def _():
        outs = _ssd_gate_norm(y_t[...].T, z_ref[...], gn_ref[...])
        gw = SSD_W // SSD_G
        for g in range(SSD_G):
            y_ref[:, g * gw:(g + 1) * gw] = outs[g].astype(BF16)


def _ssd_sample(proj, cst_t, st, cw, cb, aexp, dexp, gn, expand, *, row0, bs):
    n_steps = bs // SSD_SB
    rb = row0 // bs
    kern = functools.partial(_ssd_sample_kernel, n_steps=n_steps)
    full = lambda shape: pl.BlockSpec(shape, lambda i: (0,) * len(shape))
    return pl.pallas_call(
        kern,
        grid=(n_steps,),
        in_specs=[pl.BlockSpec((bs, 512), lambda i: (rb, COL_Z // 512)),
                  pl.BlockSpec((bs, 512), lambda i: (rb, COL_X // 512)),
                  pl.BlockSpec((bs, 512), lambda i: (rb, COL_BC // 512)),
                  pl.BlockSpec((bs, 128), lambda i: (rb, COL_DTF // 128)),
                  full((3, bs, 1024)),
                  pl.BlockSpec((SSD_SB, SSD_HEADS, SSD_P, SSD_N), lambda i: (i, 0, 0, 0)),
                  full((4, 1024)), full((1, 1024)), full((1, 512)), full((1, 512)), full((1, 512)),
                  full((128, 512))],
        out_specs=[full((bs, SSD_W)),
                   full((3, bs, 1024)),
                   pl.BlockSpec((SSD_SB, SSD_HEADS, SSD_P, SSD_N), lambda i: (i, 0, 0, 0))],
        out_shape=[jax.ShapeDtypeStruct((bs, SSD_W), BF16),
                   jax.ShapeDtypeStruct((3, bs, 1024), F32),
                   jax.ShapeDtypeStruct(st.shape, F32)],
        scratch_shapes=[pltpu.VMEM((SSD_W, bs), F32), pltpu.VMEM((SSD_W, bs), F32),
                        pltpu.VMEM((bs, SSD_G * SSD_N), F32), pltpu.VMEM((bs, SSD_G * SSD_N), F32),
                        pltpu.VMEM((SSD_W, bs), F32)],
        compiler_params=_cparams(("arbitrary",)),
        name="ssd_sample",
    )(proj, proj, proj, proj, cst_t, st, cw, cb, aexp, dexp, gn, expand)


def _s5_param_kernel(are_ref, aim_ref, ls_ref, bre_ref, bim_ref,
                     lre_ref, lim_ref, pre_ref, pim_ref, bbre_ref, bbim_ref):
    a_re, a_im = are_ref[...], aim_ref[...]
    step = jnp.exp(ls_ref[...])
    mag = jnp.exp(a_re * step)
    lr = mag * jnp.cos(a_im * step)
    li = mag * jnp.sin(a_im * step)
    den = a_re * a_re + a_im * a_im
    zr = ((lr - 1.0) * a_re + li * a_im) / den
    zi = (li * a_re - (lr - 1.0) * a_im) / den
    lre_ref[...] = lr
    lim_ref[...] = li
    pr, pi = lr, li
    for k in range(8):
        pre_ref[k] = pr
        pim_ref[k] = pi
        pr, pi = pr * lr - pi * li, pr * li + pi * lr
    for g in range(S5_G):
        bbre_ref[g] = zr[g:g + 1, :] * bre_ref[g] - zi[g:g + 1, :] * bim_ref[g]
        bbim_ref[g] = zr[g:g + 1, :] * bim_ref[g] + zi[g:g + 1, :] * bre_ref[g]


def _s5_params(a_re, a_im, log_step, b_re_t, b_im_t):
    gn = (S5_G, S5_N)
    return pl.pallas_call(
        _s5_param_kernel,
        out_shape=[jax.ShapeDtypeStruct(gn, F32), jax.ShapeDtypeStruct(gn, F32),
                   jax.ShapeDtypeStruct((8,) + gn, F32), jax.ShapeDtypeStruct((8,) + gn, F32),
                   jax.ShapeDtypeStruct((S5_G, S5_C, S5_N), F32), jax.ShapeDtypeStruct((S5_G, S5_C, S5_N), F32)],
        name="s5_params",
    )(a_re, a_im, log_step, b_re_t, b_im_t)


def _gelu_tanh(x):
    return 0.5 * x * (1.0 + jnp.tanh(math.sqrt(2.0 / math.pi) * (x + 0.044715 * (x * x * x))))


def _s5_tail(u, s_re, s_im, cdre_ref, cdim_ref, d_ref, wglu_ref, bglu_ref):
    ys = (jnp.dot(s_re.astype(BF16), cdre_ref[...], preferred_element_type=F32)
          - jnp.dot(s_im.astype(BF16), cdim_ref[...], preferred_element_type=F32)
          + d_ref[...] * u)
    g5 = _gelu_tanh(ys)
    gate = jnp.dot(g5.astype(BF16), wglu_ref[...], preferred_element_type=F32) + bglu_ref[...]
    return (g5 * jax.nn.sigmoid(gate)).astype(BF16)


def _s5_scan_kernel(u_ref, bdre_ref, bdim_ref, pre_ref, pim_ref, cdre_ref, cdim_ref, d_ref, wglu_ref, bglu_ref,
                    y_ref, sre_out, sim_out, sre, sim, car_re, car_im, *, l_real, n_chunks):
    c = pl.program_id(1)
    q = CHUNK

    @pl.when(c == 0)
    def _():
        car_re[...] = jnp.zeros(car_re.shape, F32)
        car_im[...] = jnp.zeros(car_im.shape, F32)

    u = u_ref[...]
    ub = u.astype(BF16)
    sre[...] = jnp.dot(ub, bdre_ref[...], preferred_element_type=F32)
    sim[...] = jnp.dot(ub, bdim_ref[...], preferred_element_type=F32)
    pwr, pwi = pre_ref[...], pim_ref[...]
    row = lax.broadcasted_iota(jnp.int32, (8, S5_S), 0)

    def tile_body(t, carry):
        cr, ci = carry
        r0 = pl.multiple_of(t * 8, 8)
        xr = sre[pl.ds(r0, 8), :]
        xi = sim[pl.ds(r0, 8), :]
        for s in (1, 2, 4):
            pr, pi = pwr[s - 1:s, :], pwi[s - 1:s, :]
            sr = pltpu.roll(xr, s, 0)
            si = pltpu.roll(xi, s, 0)
            m = row >= s
            xr, xi = (xr + jnp.where(m, pr * sr - pi * si, 0.0),
                      xi + jnp.where(m, pr * si + pi * sr, 0.0))
        xr, xi = xr + pwr * cr - pwi * ci, xi + pwr * ci + pwi * cr
        sre[pl.ds(r0, 8), :] = xr
        sim[pl.ds(r0, 8), :] = xi
        return xr[7:8, :], xi[7:8, :]

    cr, ci = lax.fori_loop(0, q // 8, tile_body, (car_re[0:1, :], car_im[0:1, :]))
    car_re[0:1, :] = cr
    car_im[0:1, :] = ci

    @pl.when(c == n_chunks - 1)
    def _():
        r_last = l_real - 1 - q * (n_chunks - 1)
        sre_out[...] = sre[r_last:r_last + 1, :]
        sim_out[...] = sim[r_last:r_last + 1, :]

    y_ref[...] = _s5_tail(u, sre[...], sim[...], cdre_ref, cdim_ref, d_ref, wglu_ref, bglu_ref)


def _s5_step_kernel(u_ref, s0re_ref, s0im_ref, bdre_ref, bdim_ref, pre_ref, pim_ref, cdre_ref, cdim_ref, d_ref,
                    wglu_ref, bglu_ref, y_ref, sre_out, sim_out):
    u = u_ref[...]
    ub = u.astype(BF16)
    lr, li = pre_ref[0:1, :], pim_ref[0:1, :]
    s0r, s0i = s0re_ref[...], s0im_ref[...]
    s_re = jnp.dot(ub, bdre_ref[...], preferred_element_type=F32) + lr * s0r - li * s0i
    s_im = jnp.dot(ub, bdim_ref[...], preferred_element_type=F32) + lr * s0i + li * s0r
    sre_out[...] = s_re
    sim_out[...] = s_im
    y_ref[...] = _s5_tail(u, s_re, s_im, cdre_ref, cdim_ref, d_ref, wglu_ref, bglu_ref)


def _s5_const_specs(nd):
    z = (0,) * 2
    full = lambda shape: pl.BlockSpec(shape, (lambda *_: z))
    return [full((S5_W, S5_S)), full((S5_W, S5_S)), full((8, S5_S)), full((8, S5_S)),
            full((S5_S, S5_W)), full((S5_S, S5_W)), full((1, S5_W)), full((S5_W, S5_W)), full((1, S5_W))]


def _s5_prompt(proj, consts, *, bp, lp, l_real):
    nc = lp // CHUNK
    kern = functools.partial(_s5_scan_kernel, l_real=l_real, n_chunks=nc)
    return pl.pallas_call(
        kern,
        grid=(bp, nc),
        in_specs=[pl.BlockSpec((CHUNK, 512), lambda b, c: (b * nc + c, COL_U // 512))] + _s5_const_specs(2),
        out_specs=[pl.BlockSpec((CHUNK, S5_W), lambda b, c: (b * nc + c, 0)),
                   pl.BlockSpec((None, 1, S5_S), lambda b, c: (b, 0, 0)),
                   pl.BlockSpec((None, 1, S5_S), lambda b, c: (b, 0, 0))],
        out_shape=[jax.ShapeDtypeStruct((bp * lp, S5_W), BF16),
                   jax.ShapeDtypeStruct((bp, 1, S5_S), F32),
                   jax.ShapeDtypeStruct((bp, 1, S5_S), F32)],
        scratch_shapes=[pltpu.VMEM((CHUNK, S5_S), F32), pltpu.VMEM((CHUNK, S5_S), F32),
                        pltpu.VMEM((8, S5_S), F32), pltpu.VMEM((8, S5_S), F32)],
        compiler_params=_cparams(("parallel", "arbitrary")),
        name="s5_prompt",
    )(proj, *consts)


def _s5_sample(proj, s0_re, s0_im, consts, *, row0, bs):
    rb = row0 // bs
    return pl.pallas_call(
        _s5_step_kernel,
        grid=(1,),
        in_specs=[pl.BlockSpec((bs, 512), lambda i: (rb, COL_U // 512)),
                  pl.BlockSpec((bs, S5_S), lambda i: (0, 0)),
                  pl.BlockSpec((bs, S5_S), lambda i: (0, 0))] + _s5_const_specs(1),
        out_specs=[pl.BlockSpec((bs, S5_W), lambda i: (0, 0)),
                   pl.BlockSpec((bs, S5_S), lambda i: (0, 0)),
                   pl.BlockSpec((bs, S5_S), lambda i: (0, 0))],
        out_shape=[jax.ShapeDtypeStruct((bs, S5_W), BF16),
                   jax.ShapeDtypeStruct((bs, S5_S), F32),
                   jax.ShapeDtypeStruct((bs, S5_S), F32)],
        compiler_params=_cparams(("arbitrary",)),
        name="s5_sample",
    )(proj, s0_re, s0_im, *consts)


ATT_TQ = 384


def _fox_prompt_kernel(q_ref, k_ref, v_ref, c_ref, o_ref, kb, vb):
    qi = pl.program_id(2)
    t = ATT_TQ

    @pl.when(qi == 0)
    def _():
        kb[...] = k_ref[...].astype(BF16)
        vb[...] = v_ref[...].astype(BF16)

    q = q_ref[...].astype(BF16)
    scale = ATT_D ** -0.5

    def block(j, carry, masked):
        m, l, acc = carry
        start = pl.multiple_of(j * t, 128)
        s = lax.dot_general(q, kb[pl.ds(start, t), :], (((1,), (1,)), ((), ())), preferred_element_type=F32)
        s = s * scale - c_ref[:, pl.ds(start, t)]
        if masked:
            row = lax.broadcasted_iota(jnp.int32, (t, t), 0)
            col = lax.broadcasted_iota(jnp.int32, (t, t), 1)
            s = jnp.where(row >= col, s, -jnp.inf)
        m_new = jnp.maximum(m, jnp.max(s, axis=-1, keepdims=True))
        alpha = jnp.exp(m - m_new)
        p = jnp.exp(s - m_new)
        l = alpha * l + jnp.sum(p, axis=-1, keepdims=True)
        acc = alpha * acc + jnp.dot(p.astype(BF16), vb[pl.ds(start, t), :], preferred_element_type=F32)
        return m_new, l, acc

    init = (jnp.full((t, 1), -jnp.inf, F32), jnp.zeros((t, 1), F32), jnp.zeros((t, ATT_D), F32))
    carry = lax.fori_loop(0, qi, lambda j, cr: block(j, cr, False), init)
    m, l, acc = block(qi, carry, True)
    o_ref[...] = (acc / l).astype(BF16)


def _fox_prompt(proj, catt, *, bp, lp):
    nq = lp // ATT_TQ
    return pl.pallas_call(
        _fox_prompt_kernel,
        grid=(bp, ATT_H, nq),
        in_specs=[pl.BlockSpec((ATT_TQ, ATT_D), lambda b, h, i: (b * nq + i, COL_Q // ATT_D + h)),
                  pl.BlockSpec((lp, ATT_D), lambda b, h, i: (b, COL_K // ATT_D + h)),
                  pl.BlockSpec((lp, ATT_D), lambda b, h, i: (b, COL_V // ATT_D + h)),
                  pl.BlockSpec((None, None, 1, lp), lambda b, h, i: (b, h, 0, 0))],
        out_specs=pl.BlockSpec((ATT_TQ, ATT_D), lambda b, h, i: (b * nq + i, h)),
        out_shape=jax.ShapeDtypeStruct((bp * lp, ATT_W), BF16),
        scratch_shapes=[pltpu.VMEM((lp, ATT_D), BF16), pltpu.VMEM((lp, ATT_D), BF16)],
        compiler_params=_cparams(("parallel", "parallel", "arbitrary")),
        name="fox_prompt",
    )(proj, proj, proj, catt.reshape(bp, ATT_H, 1, lp))


def _fox_sample_kernel(pt_ref, q_ref, k_ref, v_ref, lfn_ref, ck_hbm, cv_hbm, cl_hbm, o_ref,
                       kbuf, vbuf, lbuf, lg, sems, *, n_seq, n_pages):
    b = pl.program_id(0)
    slot = b % 2
    flat = PAGE * ATT_H

    def copies(seq, sl):
        out = []
        for j in range(n_pages):
            page = pt_ref[seq, j]
            out.append(pltpu.make_async_copy(ck_hbm.at[page], kbuf.at[sl, j], sems.at[sl, 0]))
            out.append(pltpu.make_async_copy(cv_hbm.at[page], vbuf.at[sl, j], sems.at[sl, 1]))
            out.append(pltpu.make_async_copy(cl_hbm.at[pl.ds(page, 1)], lbuf.at[sl, pl.ds(j, 1)], sems.at[sl, 2]))
        return out

    @pl.when(b == 0)
    def _():
        for cp in copies(b, slot):
            cp.start()

    @pl.when(b + 1 < n_seq)
    def _():
        for cp in copies(b + 1, 1 - slot):
            cp.start()

    for cp in copies(b, slot):
        cp.wait()

    q = q_ref[...]
    qb = q.astype(BF16)
    scale = ATT_D ** -0.5
    lf = lbuf[slot]
    lane = lax.broadcasted_iota(jnp.int32, (n_pages, flat), 1)
    incl = lf
    tot = lf
    sh = ATT_H
    while sh < flat:
        incl = incl + jnp.where(lane + sh < flat, pltpu.roll(incl, flat - sh, 1), 0.0)
        tot = tot + pltpu.roll(tot, sh, 1)
        sh *= 2
    pj = lax.broadcasted_iota(jnp.int32, (n_pages, n_pages), 0)
    pk = lax.broadcasted_iota(jnp.int32, (n_pages, n_pages), 1)
    later = jnp.dot((pk > pj).astype(F32), tot, precision=HIGHEST, preferred_element_type=F32)
    bias = incl - lf + later

    hrow = lax.broadcasted_iota(jnp.int32, (ATT_H, flat), 0)
    hlane = lax.broadcasted_iota(jnp.int32, (ATT_H, flat), 1)
    own = (hlane % ATT_H) == hrow
    m = jnp.sum(q * k_ref[...], axis=-1, keepdims=True) * scale
    s_self = m
    for j in range(n_pages):
        s = lax.dot_general(qb, kbuf[slot, j].astype(BF16), (((1,), (1,)), ((), ())),
                            preferred_element_type=F32)
        s = jnp.where(own, s * scale + bias[j:j + 1, :] + lfn_ref[...], -jnp.inf)
        lg[j] = s
        m = jnp.maximum(m, jnp.max(s, axis=-1, keepdims=True))
    p_self = jnp.exp(s_self - m)
    l = p_self
    acc = p_self * v_ref[...]
    for j in range(n_pages):
        p = jnp.exp(lg[j] - m)
        l = l + jnp.sum(p, axis=-1, keepdims=True)
        acc = acc + jnp.dot(p.astype(BF16), vbuf[slot, j].astype(BF16), preferred_element_type=F32)
    o_ref[...] = acc / l


def _fox_sample(page_table, q3, k3, v3, lfn3, ck, cv, cl):
    n_seq, n_pages = page_table.shape
    flat = PAGE * ATT_H
    kern = functools.partial(_fox_sample_kernel, n_seq=n_seq, n_pages=n_pages)
    seq_spec = pl.BlockSpec((None, ATT_H, ATT_D), lambda b, pt: (b, 0, 0))
    return pl.pallas_call(
        kern,
        grid_spec=pltpu.PrefetchScalarGridSpec(
            num_scalar_prefetch=1,
            grid=(n_seq,),
            in_specs=[seq_spec, seq_spec, seq_spec, pl.BlockSpec((None, ATT_H, 1), lambda b, pt: (b, 0, 0)),
                      pl.BlockSpec(memory_space=pl.ANY), pl.BlockSpec(memory_space=pl.ANY),
                      pl.BlockSpec(memory_space=pl.ANY)],
            out_specs=pl.BlockSpec((None, ATT_H, ATT_D), lambda b, pt: (b, 0, 0)),
            scratch_shapes=[pltpu.VMEM((2, n_pages, flat, ATT_D), F32),
                            pltpu.VMEM((2, n_pages, flat, ATT_D), F32),
                            pltpu.VMEM((2, n_pages, flat), F32),
                            pltpu.VMEM((n_pages, ATT_H, flat), F32),
                            pltpu.SemaphoreType.DMA((2, 3))]),
        out_shape=jax.ShapeDtypeStruct((n_seq, ATT_H, ATT_D), F32),
        compiler_params=_cparams(("arbitrary",)),
        name="fox_sample",
    )(page_table, q3, k3, v3, lfn3, ck, cv, cl)


def _outproj_kernel(h_ref, ya_ref, yb_ref, yc_ref, sa_ref, sb_ref, sc_ref, w_ref, o_ref, *, n_prompt_tiles):
    i = pl.program_id(0)

    def mix(a, b, c):
        return (jnp.dot(a, w_ref[0:SSD_W, :], preferred_element_type=F32)
                + jnp.dot(b, w_ref[SSD_W:SSD_W + S5_W, :], preferred_element_type=F32)
                + jnp.dot(c, w_ref[SSD_W + S5_W:, :], preferred_element_type=F32))

    @pl.when(i < n_prompt_tiles)
    def _():
        o_ref[...] = h_ref[...] + mix(ya_ref[...], yb_ref[...], yc_ref[...])

    @pl.when(i >= n_prompt_tiles)
    def _():
        bs = sa_ref.shape[0]
        o_ref[0:bs, :] = h_ref[0:bs, :] + mix(sa_ref[...], sb_ref[...], sc_ref[...])
        o_ref[bs:, :] = h_ref[bs:, :]


def _outproj(h, ya, yb, yc, sa, sb, sc, w, tm=256):
    r, d = h.shape
    npt = ya.shape[0] // tm
    bs = sa.shape[0]
    assert ya.shape[0] % tm == 0 and r == (npt + 1) * tm and bs <= tm
    kern = functools.partial(_outproj_kernel, n_prompt_tiles=npt)
    pidx = lambda i: (jnp.minimum(i, npt - 1), 0)
    return pl.pallas_call(
        kern,
        grid=(npt + 1,),
        in_specs=[pl.BlockSpec((tm, d), lambda i: (i, 0)),
                  pl.BlockSpec((tm, SSD_W), pidx), pl.BlockSpec((tm, S5_W), pidx), pl.BlockSpec((tm, ATT_W), pidx),
                  pl.BlockSpec((bs, SSD_W), lambda i: (0, 0)), pl.BlockSpec((bs, S5_W), lambda i: (0, 0)),
                  pl.BlockSpec((bs, ATT_W), lambda i: (0, 0)),
                  pl.BlockSpec((d, d), lambda i: (0, 0))],
        out_specs=pl.BlockSpec((tm, d), lambda i: (i, 0)),
        out_shape=jax.ShapeDtypeStruct((r, d), F32),
        compiler_params=_cparams(("parallel",)),
        name="outproj",
    )(h, ya, yb, yc, sa, sb, sc, w)


def _block_diag(blocks):
    g, a, b = blocks.shape
    eye = jnp.eye(g, dtype=blocks.dtype)
    return (eye[:, None, :, None] * blocks[:, :, None, :]).reshape(g * a, g * b)


def kernel(x_prompt, x_sample, cache_k, cache_v, cache_logf, state_ssd, state_ssd_conv, state_s5_re, state_s5_im,
           page_table, meta_tokens, ffn1_norm, ffn1_gate, ffn1_up, ffn1_down, mix_norm, w_in, ssd_conv_w,
           ssd_conv_b, ssd_dt_bias, ssd_A_log, ssd_D, ssd_norm, s5_A_re, s5_A_im, s5_log_step, s5_B_re, s5_B_im,
           s5_C_re, s5_C_im, s5_D, s5_w_glu, s5_b_glu, att_q_norm, att_k_norm, att_f_bias, w_out, ffn2_norm,
           ffn2_gate, ffn2_up, ffn2_down):
    bp, seq, d = x_prompt.shape
    bs = x_sample.shape[0]
    depth = w_in.shape[0]
    n_phys = cache_k.shape[1]
    l_real = seq + N_META
    lp = -(-l_real // (3 * CHUNK)) * (3 * CHUNK)
    row_s = bp * lp
    r_rows = row_s + 2 * bs
    assert lp % ATT_TQ == 0 and row_s % 256 == 0 and r_rows % 512 == 0 and bs == 128

    meta = meta_tokens.astype(F32)
    pad = jnp.zeros((lp - l_real, d), F32)
    pieces = []
    for b in range(bp):
        pieces += [meta, x_prompt[b], pad]
    pieces += [x_sample[:, 0, :], jnp.zeros((bs, d), F32)]
    h = jnp.concatenate(pieces, axis=0)

    expand = (jnp.arange(128)[:, None] == (jnp.arange(SSD_W)[None, :] // SSD_P)).astype(F32)
    n_tiles = PROJ_W // PROJ_TN

    outs = {k: [] for k in ("k_p", "v_p", "lf_p", "k_s", "v_s", "lf_s", "ssd_p", "ssd_s", "conv_p", "conv_s",
                            "re_p", "re_s", "im_p", "im_s")}
    for i in range(depth):
        wi = w_in[i]
        z_w, xbc_w, dt_w, u_w, q_w, k_w, v_w, f_w = jnp.split(
            wi, [512, 1536, 1544, 2056, 3080, 4104, 5128], axis=1)
        w_proj = jnp.concatenate(
            [z_w, xbc_w, u_w, q_w, k_w, v_w, dt_w, f_w,
             jnp.zeros((d, PROJ_W - COL_DTF - 2 * SSD_HEADS), F32)], axis=1).astype(BF16)
        gains = jnp.ones((n_tiles, 1, PROJ_TN), F32)
        gq = jnp.tile(att_q_norm[i], PROJ_TN // ATT_D)
        gk = jnp.tile(att_k_norm[i], PROJ_TN // ATT_D)
        gains = gains.at[COL_Q // PROJ_TN:COL_K // PROJ_TN, 0].set(gq).at[COL_K // PROJ_TN:COL_V // PROJ_TN, 0].set(gk)
        bias = jnp.zeros((1, PROJ_TN), F32).at[0, 0:8].set(ssd_dt_bias[i]).at[0, 8:16].set(att_f_bias[i])
        cw, cb = ssd_conv_w[i], ssd_conv_b[i][None, :]
        alog_row = jnp.zeros((1, 128), F32).at[0, 0:SSD_HEADS].set(ssd_A_log[i])
        aexp = jnp.repeat(ssd_A_log[i], SSD_P)[None, :]
        dexp = jnp.repeat(ssd_D[i], SSD_P)[None, :]
        gn = ssd_norm[i][None, :]

        h = _ffn(h, ffn1_norm[i][None, :], ffn1_gate[i].astype(BF16), ffn1_up[i].astype(BF16),
                 ffn1_down[i].astype(BF16))
        proj = _inproj(h, mix_norm[i][None, :], w_proj, gains, bias)

        y_ssd_p, catt, ssd_p, conv_p = _ssd_prompt(proj, cw, cb, alog_row, dexp, gn, bp=bp, lp=lp, l_real=l_real)
        y_ssd_s, conv_s, ssd_s = _ssd_sample(proj, jnp.moveaxis(state_ssd_conv[i], 1, 0), state_ssd[i],
                                             cw, cb, aexp, dexp, gn, expand, row0=row_s, bs=bs)
        lam_re, lam_im, pw_re, pw_im, bb_re, bb_im = _s5_params(
            s5_A_re[i], s5_A_im[i], s5_log_step[i][:, None],
            jnp.swapaxes(s5_B_re[i], 1, 2), jnp.swapaxes(s5_B_im[i], 1, 2))
        consts = (_block_diag(bb_re).astype(BF16), _block_diag(bb_im).astype(BF16),
                  pw_re.reshape(8, S5_S), pw_im.reshape(8, S5_S),
                  _block_diag(jnp.swapaxes(s5_C_re[i], 1, 2)).astype(BF16),
                  _block_diag(jnp.swapaxes(s5_C_im[i], 1, 2)).astype(BF16),
                  s5_D[i].reshape(1, S5_W), s5_w_glu[i].astype(BF16), s5_b_glu[i][None, :])
        y_s5_p, re_p, im_p = _s5_prompt(proj, consts, bp=bp, lp=lp, l_real=l_real)
        y_s5_s, re_s, im_s = _s5_sample(proj, state_s5_re[i].reshape(bs, S5_S), state_s5_im[i].reshape(bs, S5_S),
                                        consts, row0=row_s, bs=bs)
        y_att_p = _fox_prompt(proj, catt, bp=bp, lp=lp)
        samp = proj[row_s:row_s + bs]
        q3 = samp[:, COL_Q:COL_K].reshape(bs, ATT_H, ATT_D)
        k3 = samp[:, COL_K:COL_V].reshape(bs, ATT_H, ATT_D)
        v3 = samp[:, COL_V:COL_DTF].reshape(bs, ATT_H, ATT_D)
        lfn3 = samp[:, COL_DTF + 8:COL_DTF + 16].reshape(bs, ATT_H, 1)
        y_att_s = _fox_sample(page_table, q3, k3, v3, lfn3,
                              cache_k[i].reshape(n_phys, PAGE * ATT_H, ATT_D),
                              cache_v[i].reshape(n_phys, PAGE * ATT_H, ATT_D),
                              cache_logf[i].reshape(n_phys, PAGE * ATT_H))
        h = _outproj(h, y_ssd_p, y_s5_p, y_att_p, y_ssd_s, y_s5_s,
                     y_att_s.reshape(bs, ATT_W).astype(BF16), w_out[i].astype(BF16))
        h = _ffn(h, ffn2_norm[i][None, :], ffn2_gate[i].astype(BF16), ffn2_up[i].astype(BF16),
                 ffn2_down[i].astype(BF16))

        pp = proj[:row_s].reshape(bp, lp, PROJ_W)[:, :l_real]
        outs["k_p"].append(pp[:, :, COL_K:COL_V].reshape(bp, l_real, ATT_H, ATT_D))
        outs["v_p"].append(pp[:, :, COL_V:COL_DTF].reshape(bp, l_real, ATT_H, ATT_D))
        outs["lf_p"].append(pp[:, :, COL_DTF + 8:COL_DTF + 16])
        outs["k_s"].append(k3[:, None])
        outs["v_s"].append(v3[:, None])
        outs["lf_s"].append(samp[:, None, COL_DTF + 8:COL_DTF + 16])
        outs["ssd_p"].append(jnp.swapaxes(ssd_p, 2, 3))
        outs["ssd_s"].append(ssd_s)
        outs["conv_p"].append(conv_p)
        outs["conv_s"].append(jnp.moveaxis(conv_s, 0, 1))
        outs["re_p"].append(re_p.reshape(bp, S5_G, S5_N))
        outs["re_s"].append(re_s.reshape(bs, S5_G, S5_N))
        outs["im_p"].append(im_p.reshape(bp, S5_G, S5_N))
        outs["im_s"].append(im_s.reshape(bs, S5_G, S5_N))

    st = {k: jnp.stack(v, axis=0) for k, v in outs.items()}
    y_prompt = h[:row_s].reshape(bp, lp, d)[:, N_META:l_real]
    y_sample = h[row_s:row_s + bs][:, None, :]
    return (y_prompt, y_sample, st["k_p"], st["v_p"], st["lf_p"], st["k_s"], st["v_s"], st["lf_s"],
            st["ssd_p"], st["ssd_s"], st["conv_p"], st["conv_s"], st["re_p"], st["re_s"], st["im_p"], st["im_s"])
```

```python
import functools
import math

import jax
import jax.numpy as jnp
from jax import lax
from jax.experimental import pallas as pl
from jax.experimental.pallas import tpu as pltpu

F32 = jnp.float32
BF16 = jnp.bfloat16
HIGHEST = lax.Precision.HIGHEST

NORM_EPS = 1e-6
N_META = 16
PAGE = 128
CHUNK = 128
SSD_HEADS, SSD_P, SSD_N, SSD_G = 8, 64, 128, 2
SSD_W = SSD_HEADS * SSD_P
S5_G, S5_C, S5_N = 32, 16, 64
S5_W = S5_G * S5_C
S5_S = S5_G * S5_N
ATT_H, ATT_D = 8, 128
ATT_W = ATT_H * ATT_D
PROJ_TN = 512
COL_Z, COL_X, COL_BC, COL_U, COL_Q, COL_K, COL_V, COL_DTF = 0, 512, 1024, 1536, 2048, 3072, 4096, 5120
PROJ_W = 5632
VMEM_LIMIT = 56 * 1024 * 1024


def _cparams(sem):
    return pltpu.CompilerParams(dimension_semantics=sem, vmem_limit_bytes=VMEM_LIMIT)


def _rms(x, g):
    return x * lax.rsqrt(jnp.mean(x * x, axis=-1, keepdims=True) + NORM_EPS) * g


def _silu(x):
    return x * jax.nn.sigmoid(x)


def _softplus(x):
    return jnp.maximum(x, 0.0) + jnp.log1p(jnp.exp(-jnp.abs(x)))


def _ffn_kernel(x_ref, g_ref, wg_ref, wu_ref, wd_ref, o_ref, xn_ref):
    j = pl.program_id(1)

    @pl.when(j == 0)
    def _():
        x = x_ref[...]
        xn_ref[...] = _rms(x, g_ref[...]).astype(BF16)
        o_ref[...] = x

    xn = xn_ref[...]
    gate = jnp.dot(xn, wg_ref[...], preferred_element_type=F32)
    up = jnp.dot(xn, wu_ref[...], preferred_element_type=F32)
    act = (0.5 * _silu(gate) * up).astype(BF16)
    o_ref[...] += jnp.dot(act, wd_ref[...], preferred_element_type=F32)


def _ffn(h, g, wg, wu, wd, layer, tm=512, tf=512):
    r, d = h.shape
    f = wg.shape[2]
    return pl.pallas_call(
        _ffn_kernel,
        grid=(r // tm, f // tf),
        in_specs=[pl.BlockSpec((tm, d), lambda i, j: (i, 0)),
                  pl.BlockSpec((1, d), lambda i, j: (0, 0)),
                  pl.BlockSpec((None, d, tf), lambda i, j: (layer, 0, j)),
                  pl.BlockSpec((None, d, tf), lambda i, j: (layer, 0, j)),
                  pl.BlockSpec((None, tf, d), lambda i, j: (layer, j, 0))],
        out_specs=pl.BlockSpec((tm, d), lambda i, j: (i, 0)),
        out_shape=jax.ShapeDtypeStruct((r, d), F32),
        scratch_shapes=[pltpu.VMEM((tm, d), BF16)],
        compiler_params=_cparams(("parallel", "arbitrary")),
        name="ffn",
    )(h, g, wg, wu, wd)


def _inproj_kernel(x_ref, g_ref, w_ref, gain_ref, bias_ref, o_ref, xn_ref):
    j = pl.program_id(1)

    @pl.when(j == 0)
    def _():
        xn_ref[...] = _rms(x_ref[...], g_ref[...]).astype(BF16)

    acc = jnp.dot(xn_ref[...], w_ref[...], preferred_element_type=F32)
    jq, jv, jd = COL_Q // PROJ_TN, COL_V // PROJ_TN, COL_DTF // PROJ_TN
    is_qk = jnp.logical_and(j >= jq, j < jv)

    @pl.when(jnp.logical_and(jnp.logical_not(is_qk), j != jd))
    def _():
        o_ref[...] = acc

    @pl.when(is_qk)
    def _():
        for hh in range(PROJ_TN // ATT_D):
            sl = slice(hh * ATT_D, (hh + 1) * ATT_D)
            o_ref[:, sl] = _rms(acc[:, sl], gain_ref[:, sl])

    @pl.when(j == jd)
    def _():
        a = acc[:, :128] + bias_ref[:, :128]
        lane = lax.broadcasted_iota(jnp.int32, a.shape, 1)
        is_dt = lane < SSD_HEADS
        sp = _softplus(jnp.where(is_dt, a, -a))
        o_ref[:, :128] = jnp.where(is_dt, sp, -sp)
        o_ref[:, 128:] = jnp.zeros((acc.shape[0], PROJ_TN - 128), F32)


def _inproj(h, g, w, gains, bias, layer, tm=512):
    r, d = h.shape
    n = w.shape[2]
    tn = PROJ_TN
    return pl.pallas_call(
        _inproj_kernel,
        grid=(r // tm, n // tn),
        in_specs=[pl.BlockSpec((tm, d), lambda i, j: (i, 0)),
                  pl.BlockSpec((1, d), lambda i, j: (0, 0)),
                  pl.BlockSpec((None, d, tn), lambda i, j: (layer, 0, j)),
                  pl.BlockSpec((None, 1, tn), lambda i, j: (j, 0, 0)),
                  pl.BlockSpec((1, tn), lambda i, j: (0, 0))],
        out_specs=pl.BlockSpec((tm, tn), lambda i, j: (i, j)),
        out_shape=jax.ShapeDtypeStruct((r, n), F32),
        scratch_shapes=[pltpu.VMEM((tm, d), BF16)],
        compiler_params=_cparams(("parallel", "arbitrary")),
        name="inproj",
    )(h, g, w, gains, bias)


def _ssd_gate_norm(y, z, gn):
    yz = y * _silu(z)
    gw = SSD_W // SSD_G
    outs = []
    for g in range(SSD_G):
        blk = yz[:, g * gw:(g + 1) * gw]
        outs.append(blk * lax.rsqrt(jnp.mean(blk * blk, axis=-1, keepdims=True) + NORM_EPS)
                    * gn[:, g * gw:(g + 1) * gw])
    return outs


def _ssd_prompt_kernel(z_ref, xx_ref, bc_ref, dtf_ref, cw_ref, cb_ref, alog_ref, dexp_ref, gn_ref,
                       y_ref, catt_ref, st_ref, cst_ref, xbuf, hst, ccarry, ybuf, *, l_real, n_chunks):
    c = pl.program_id(1)
    q = CHUNK

    @pl.when(c == 0)
    def _():
        xbuf[0:8, :] = jnp.zeros((8, xbuf.shape[1]), F32)
        hst[...] = jnp.zeros(hst.shape, F32)
        ccarry[...] = jnp.zeros(ccarry.shape, F32)

    xbuf[8:8 + q, 0:SSD_W] = xx_ref[...]
    xbuf[8:8 + q, SSD_W:] = bc_ref[...]
    conv = cb_ref[...] + cw_ref[0:1, :] * xbuf[5:5 + q, :]
    for k in range(1, 4):
        conv = conv + cw_ref[k:k + 1, :] * xbuf[5 + k:5 + k + q, :]
    xc = _silu(conv)

    @pl.when(c == n_chunks - 1)
    def _():
        r0 = 8 + l_real - q * (n_chunks - 1) - 3
        cst_ref[...] = xbuf[r0:r0 + 3, :]

    xbuf[0:8, :] = xbuf[q:q + 8, :]

    row = lax.broadcasted_iota(jnp.int32, (q, q), 0)
    col = lax.broadcasted_iota(jnp.int32, (q, q), 1)
    lane1 = lax.broadcasted_iota(jnp.int32, (1, q), 1)
    valid = (c * q + row) < l_real
    dtm = jnp.where(valid, dtf_ref[...], 0.0)
    a_row = -jnp.exp(alog_ref[...])
    scale_row = jnp.where(lane1 < SSD_HEADS, a_row, jnp.where(lane1 < 2 * SSD_HEADS, 1.0, 0.0))
    causal = row >= col
    cs = jnp.dot(causal.astype(F32), dtm * scale_row, precision=HIGHEST, preferred_element_type=F32)
    cs_t = cs.T
    catt_ref[...] = (cs + ccarry[0:1, :]).T[SSD_HEADS:2 * SSD_HEADS, :]
    ccarry[0:1, :] = ccarry[0:1, :] + cs[q - 1:q, :]

    hpg = SSD_HEADS // SSD_G
    for g in range(SSD_G):
        bg = xc[:, SSD_W + g * SSD_N:SSD_W + (g + 1) * SSD_N]
        cg = xc[:, SSD_W + (SSD_G + g) * SSD_N:SSD_W + (SSD_G + g + 1) * SSD_N].astype(BF16)
        bg_t = bg.T.astype(BF16)
        cb = jnp.dot(cg, bg_t, preferred_element_type=F32)
        for hh in range(hpg):
            h = g * hpg + hh
            xh = xc[:, h * SSD_P:(h + 1) * SSD_P]
            acs_col = cs[:, h:h + 1]
            seg = acs_col - cs_t[h:h + 1, :]
            decay = jnp.exp(jnp.where(causal, seg, -jnp.inf))
            xdt = xh * dtm[:, h:h + 1]
            y = jnp.dot((cb * decay).astype(BF16), xdt.astype(BF16), preferred_element_type=F32)
            h_t = hst[h]
            y = y + jnp.dot(cg, h_t.astype(BF16), preferred_element_type=F32) * jnp.exp(acs_col)
            a_last = cs[q - 1:q, h:h + 1]
            xw = (xdt * jnp.exp(a_last - acs_col)).astype(BF16)
            hst[h] = h_t * jnp.exp(a_last) + jnp.dot(bg_t, xw, preferred_element_type=F32)
            ybuf[:, h * SSD_P:(h + 1) * SSD_P] = y + dexp_ref[:, h * SSD_P:(h + 1) * SSD_P] * xh

    outs = _ssd_gate_norm(ybuf[...], z_ref[...], gn_ref[...])
    gw = SSD_W // SSD_G
    for g in range(SSD_G):
        y_ref[:, g * gw:(g + 1) * gw] = outs[g].astype(BF16)

    @pl.when(c == n_chunks - 1)
    def _():
        st_ref[...] = hst[...]


def _ssd_prompt(proj, cw, cb, alog_row, dexp, gn, *, bp, lp, l_real):
    nc = lp // CHUNK
    cw_b = SSD_W // 512
    kern = functools.partial(_ssd_prompt_kernel, l_real=l_real, n_chunks=nc)
    rowblk = lambda b, c: b * nc + c
    return pl.pallas_call(
        kern,
        grid=(bp, nc),
        in_specs=[pl.BlockSpec((CHUNK, 512), lambda b, c: (rowblk(b, c), COL_Z // 512)),
                  pl.BlockSpec((CHUNK, 512), lambda b, c: (rowblk(b, c), COL_X // 512)),
                  pl.BlockSpec((CHUNK, 512), lambda b, c: (rowblk(b, c), COL_BC // 512)),
                  pl.BlockSpec((CHUNK, 128), lambda b, c: (rowblk(b, c), COL_DTF // 128)),
                  pl.BlockSpec((4, 1024), lambda b, c: (0, 0)),
                  pl.BlockSpec((1, 1024), lambda b, c: (0, 0)),
                  pl.BlockSpec((1, 128), lambda b, c: (0, 0)),
                  pl.BlockSpec((1, 512), lambda b, c: (0, 0)),
                  pl.BlockSpec((1, 512), lambda b, c: (0, 0))],
        out_specs=[pl.BlockSpec((CHUNK, SSD_W), lambda b, c: (rowblk(b, c), 0)),
                   pl.BlockSpec((None, SSD_HEADS, CHUNK), lambda b, c: (b, 0, c)),
                   pl.BlockSpec((None, SSD_HEADS, SSD_N, SSD_P), lambda b, c: (b, 0, 0, 0)),
                   pl.BlockSpec((None, 3, 1024), lambda b, c: (b, 0, 0))],
        out_shape=[jax.ShapeDtypeStruct((bp * lp, SSD_W), BF16),
                   jax.ShapeDtypeStruct((bp, ATT_H, lp), F32),
                   jax.ShapeDtypeStruct((bp, SSD_HEADS, SSD_N, SSD_P), F32),
                   jax.ShapeDtypeStruct((bp, 3, 1024), F32)],
        scratch_shapes=[pltpu.VMEM((CHUNK + 8, 1024), F32),
                        pltpu.VMEM((SSD_HEADS, SSD_N, SSD_P), F32),
                        pltpu.VMEM((8, 128), F32),
                        pltpu.VMEM((CHUNK, SSD_W), F32)],
        compiler_params=_cparams(("parallel", "arbitrary")),
        name="ssd_prompt",
    )(proj, proj, proj, proj, cw, cb, alog_row, dexp, gn)


SSD_SB = 16


def _ssd_sample_kernel(z_ref, xx_ref, bc_ref, dtf_ref, cst_ref, st_ref, cw_ref, cb_ref, aexp_ref, dexp_ref,
                       gn_ref, expand_ref, y_ref, cst_out, st_out, dec_t, xdt_t, bm_s, cm_s, y_t, *, n_steps):
    i = pl.program_id(0)
    bs = z_ref.shape[0]

    @pl.when(i == 0)
    def _():
        w = cw_ref[...]
        for half, ref in ((0, xx_ref), (1, bc_ref)):
            sl = slice(half * 512, (half + 1) * 512)
            xn = ref[...]
            conv = (cb_ref[:, sl] + w[0:1, sl] * cst_ref[0, :, sl] + w[1:2, sl] * cst_ref[1, :, sl]
                    + w[2:3, sl] * cst_ref[2, :, sl] + w[3:4, sl] * xn)
            xcv = _silu(conv)
            cst_out[0, :, sl] = cst_ref[1, :, sl]
            cst_out[1, :, sl] = cst_ref[2, :, sl]
            cst_out[2, :, sl] = xn
            if half == 0:
                dt_e = jnp.dot(dtf_ref[...], expand_ref[...], precision=HIGHEST, preferred_element_type=F32)
                dec_t[...] = jnp.exp(dt_e * (-jnp.exp(aexp_ref[...]))).T
                xdt_t[...] = (xcv * dt_e).T
                y_t[...] = (dexp_ref[...] * xcv).T
            else:
                bm_s[...] = xcv[:, 0:SSD_G * SSD_N]
                cm_s[...] = xcv[:, SSD_G * SSD_N:]

    shift = (bs - i * SSD_SB) % bs
    dec_blk = pltpu.roll(dec_t[...], shift, 1)
    xdt_blk = pltpu.roll(xdt_t[...], shift, 1)
    lane = lax.broadcasted_iota(jnp.int32, (1, bs), 1)
    hpg = SSD_HEADS // SSD_G
    yacc = [jnp.zeros((SSD_P, bs), F32) for _ in range(SSD_HEADS)]
    for bb in range(SSD_SB):
        onehot = (lane == bb).astype(F32)
        brow = bm_s[pl.ds(i * SSD_SB + bb, 1), :]
        crow = cm_s[pl.ds(i * SSD_SB + bb, 1), :]
        for h in range(SSD_HEADS):
            g = h // hpg
            rs = slice(h * SSD_P, (h + 1) * SSD_P)
            hn = (st_ref[bb, h] * dec_blk[rs, bb:bb + 1]
                  + xdt_blk[rs, bb:bb + 1] * brow[:, g * SSD_N:(g + 1) * SSD_N])
            st_out[bb, h] = hn
            ycol = jnp.sum(hn * crow[:, g * SSD_N:(g + 1) * SSD_N], axis=-1, keepdims=True)
            yacc[h] = yacc[h] + ycol * onehot
    for h in range(SSD_HEADS):
        rs = slice(h * SSD_P, (h + 1) * SSD_P)
        y_t[rs, :] = y_t[rs, :] + pltpu.roll(yacc[h], i * SSD_SB, 1)

    @pl.when(i == n_steps - 1)
    def _():
        outs = _ssd_gate_norm(y_t[...].T, z_ref[...], gn_ref[...])
        gw = SSD_W // SSD_G
        for g in range(SSD_G):
            y_ref[:, g * gw:(g + 1) * gw] = outs[g].astype(BF16)


def _ssd_sample(proj, cst_t, st, cw, cb, aexp, dexp, gn, expand, *, row0, bs, layer):
    n_steps = bs // SSD_SB
    rb = row0 // bs
    kern = functools.partial(_ssd_sample_kernel, n_steps=n_steps)
    full = lambda shape: pl.BlockSpec(shape, lambda i: (0,) * len(shape))
    return pl.pallas_call(
        kern,
        grid=(n_steps,),
        in_specs=[pl.BlockSpec((bs, 512), lambda i: (rb, COL_Z // 512)),
                  pl.BlockSpec((bs, 512), lambda i: (rb, COL_X // 512)),
                  pl.BlockSpec((bs, 512), lambda i: (rb, COL_BC // 512)),
                  pl.BlockSpec((bs, 128), lambda i: (rb, COL_DTF // 128)),
                  full((3, bs, 1024)),
                  pl.BlockSpec((None, SSD_SB, SSD_HEADS, SSD_P, SSD_N), lambda i: (layer, i, 0, 0, 0)),
                  full((4, 1024)), full((1, 1024)), full((1, 512)), full((1, 512)), full((1, 512)),
                  full((128, 512))],
        out_specs=[full((bs, SSD_W)),
                   full((3, bs, 1024)),
                   pl.BlockSpec((SSD_SB, SSD_HEADS, SSD_P, SSD_N), lambda i: (i, 0, 0, 0))],
        out_shape=[jax.ShapeDtypeStruct((bs, SSD_W), BF16),
                   jax.ShapeDtypeStruct((3, bs, 1024), F32),
                   jax.ShapeDtypeStruct(st.shape[1:], F32)],
        scratch_shapes=[pltpu.VMEM((SSD_W, bs), F32), pltpu.VMEM((SSD_W, bs), F32),
                        pltpu.VMEM((bs, SSD_G * SSD_N), F32), pltpu.VMEM((bs, SSD_G * SSD_N), F32),
                        pltpu.VMEM((SSD_W, bs), F32)],
        compiler_params=_cparams(("arbitrary",)),
        name="ssd_sample",
    )(proj, proj, proj, proj, cst_t, st, cw, cb, aexp, dexp, gn, expand)


def _s5_param_kernel(are_ref, aim_ref, ls_ref, bre_ref, bim_ref,
                     lre_ref, lim_ref, pre_ref, pim_ref, bbre_ref, bbim_ref):
    a_re, a_im = are_ref[...], aim_ref[...]
    step = jnp.exp(ls_ref[...])
    mag = jnp.exp(a_re * step)
    lr = mag * jnp.cos(a_im * step)
    li = mag * jnp.sin(a_im * step)
    den = a_re * a_re + a_im * a_im
    zr = ((lr - 1.0) * a_re + li * a_im) / den
    zi = (li * a_re - (lr - 1.0) * a_im) / den
    lre_ref[...] = lr
    lim_ref[...] = li
    pr, pi = lr, li
    for k in range(8):
        pre_ref[k] = pr
        pim_ref[k] = pi
        pr, pi = pr * lr - pi * li, pr * li + pi * lr
    for g in range(S5_G):
        bbre_ref[g] = zr[g:g + 1, :] * bre_ref[g] - zi[g:g + 1, :] * bim_ref[g]
        bbim_ref[g] = zr[g:g + 1, :] * bim_ref[g] + zi[g:g + 1, :] * bre_ref[g]


def _s5_params(a_re, a_im, log_step, b_re_t, b_im_t):
    gn = (S5_G, S5_N)
    return pl.pallas_call(
        _s5_param_kernel,
        out_shape=[jax.ShapeDtypeStruct(gn, F32), jax.ShapeDtypeStruct(gn, F32),
                   jax.ShapeDtypeStruct((8,) + gn, F32), jax.ShapeDtypeStruct((8,) + gn, F32),
                   jax.ShapeDtypeStruct((S5_G, S5_C, S5_N), F32), jax.ShapeDtypeStruct((S5_G, S5_C, S5_N), F32)],
        name="s5_params",
    )(a_re, a_im, log_step, b_re_t, b_im_t)


def _gelu_tanh(x):
    return 0.5 * x * (1.0 + jnp.tanh(math.sqrt(2.0 / math.pi) * (x + 0.044715 * (x * x * x))))


def _s5_tail(u, s_re, s_im, cdre_ref, cdim_ref, d_ref, wglu_ref, bglu_ref):
    ys = (jnp.dot(s_re.astype(BF16), cdre_ref[...], preferred_element_type=F32)
          - jnp.dot(s_im.astype(BF16), cdim_ref[...], preferred_element_type=F32)
          + d_ref[...] * u)
    g5 = _gelu_tanh(ys)
    gate = jnp.dot(g5.astype(BF16), wglu_ref[...], preferred_element_type=F32) + bglu_ref[...]
    return (g5 * jax.nn.sigmoid(gate)).astype(BF16)


def _s5_scan_kernel(u_ref, bdre_ref, bdim_ref, pre_ref, pim_ref, cdre_ref, cdim_ref, d_ref, wglu_ref, bglu_ref,
                    y_ref, sre_out, sim_out, sre, sim, car_re, car_im, *, l_real, n_chunks):
    c = pl.program_id(1)
    q = CHUNK

    @pl.when(c == 0)
    def _():
        car_re[...] = jnp.zeros(car_re.shape, F32)
        car_im[...] = jnp.zeros(car_im.shape, F32)

    u = u_ref[...]
    ub = u.astype(BF16)
    sre[...] = jnp.dot(ub, bdre_ref[...], preferred_element_type=F32)
    sim[...] = jnp.dot(ub, bdim_ref[...], preferred_element_type=F32)
    pwr, pwi = pre_ref[...], pim_ref[...]
    row = lax.broadcasted_iota(jnp.int32, (8, S5_S), 0)

    def tile_body(t, carry):
        cr, ci = carry
        r0 = pl.multiple_of(t * 8, 8)
        xr = sre[pl.ds(r0, 8), :]
        xi = sim[pl.ds(r0, 8), :]
        for s in (1, 2, 4):
            pr, pi = pwr[s - 1:s, :], pwi[s - 1:s, :]
            sr = pltpu.roll(xr, s, 0)
            si = pltpu.roll(xi, s, 0)
            m = row >= s
            xr, xi = (xr + jnp.where(m, pr * sr - pi * si, 0.0),
                      xi + jnp.where(m, pr * si + pi * sr, 0.0))
        xr, xi = xr + pwr * cr - pwi * ci, xi + pwr * ci + pwi * cr
        sre[pl.ds(r0, 8), :] = xr
        sim[pl.ds(r0, 8), :] = xi
        return xr[7:8, :], xi[7:8, :]

    cr, ci = lax.fori_loop(0, q // 8, tile_body, (car_re[0:1, :], car_im[0:1, :]))
    car_re[0:1, :] = cr
    car_im[0:1, :] = ci

    @pl.when(c == n_chunks - 1)
    def _():
        r_last = l_real - 1 - q * (n_chunks - 1)
        sre_out[...] = sre[r_last:r_last + 1, :]
        sim_out[...] = sim[r_last:r_last + 1, :]

    y_ref[...] = _s5_tail(u, sre[...], sim[...], cdre_ref, cdim_ref, d_ref, wglu_ref, bglu_ref)


def _s5_step_kernel(u_ref, s0re_ref, s0im_ref, bdre_ref, bdim_ref, pre_ref, pim_ref, cdre_ref, cdim_ref, d_ref,
                    wglu_ref, bglu_ref, y_ref, sre_out, sim_out):
    u = u_ref[...]
    ub = u.astype(BF16)
    lr, li = pre_ref[0:1, :], pim_ref[0:1, :]
    s0r, s0i = s0re_ref[...], s0im_ref[...]
    s_re = jnp.dot(ub, bdre_ref[...], preferred_element_type=F32) + lr * s0r - li * s0i
    s_im = jnp.dot(ub, bdim_ref[...], preferred_element_type=F32) + lr * s0i + li * s0r
    sre_out[...] = s_re
    sim_out[...] = s_im
    y_ref[...] = _s5_tail(u, s_re, s_im, cdre_ref, cdim_ref, d_ref, wglu_ref, bglu_ref)


def _s5_const_specs(nd):
    z = (0,) * 2
    full = lambda shape: pl.BlockSpec(shape, (lambda *_: z))
    return [full((S5_W, S5_S)), full((S5_W, S5_S)), full((8, S5_S)), full((8, S5_S)),
            full((S5_S, S5_W)), full((S5_S, S5_W)), full((1, S5_W)), full((S5_W, S5_W)), full((1, S5_W))]


def _s5_prompt(proj, consts, *, bp, lp, l_real):
    nc = lp // CHUNK
    kern = functools.partial(_s5_scan_kernel, l_real=l_real, n_chunks=nc)
    return pl.pallas_call(
        kern,
        grid=(bp, nc),
        in_specs=[pl.BlockSpec((CHUNK, 512), lambda b, c: (b * nc + c, COL_U // 512))] + _s5_const_specs(2),
        out_specs=[pl.BlockSpec((CHUNK, S5_W), lambda b, c: (b * nc + c, 0)),
                   pl.BlockSpec((None, 1, S5_S), lambda b, c: (b, 0, 0)),
                   pl.BlockSpec((None, 1, S5_S), lambda b, c: (b, 0, 0))],
        out_shape=[jax.ShapeDtypeStruct((bp * lp, S5_W), BF16),
                   jax.ShapeDtypeStruct((bp, 1, S5_S), F32),
                   jax.ShapeDtypeStruct((bp, 1, S5_S), F32)],
        scratch_shapes=[pltpu.VMEM((CHUNK, S5_S), F32), pltpu.VMEM((CHUNK, S5_S), F32),
                        pltpu.VMEM((8, S5_S), F32), pltpu.VMEM((8, S5_S), F32)],
        compiler_params=_cparams(("parallel", "arbitrary")),
        name="s5_prompt",
    )(proj, *consts)


def _s5_sample(proj, s0_re, s0_im, consts, *, row0, bs):
    rb = row0 // bs
    return pl.pallas_call(
        _s5_step_kernel,
        grid=(1,),
        in_specs=[pl.BlockSpec((bs, 512), lambda i: (rb, COL_U // 512)),
                  pl.BlockSpec((bs, S5_S), lambda i: (0, 0)),
                  pl.BlockSpec((bs, S5_S), lambda i: (0, 0))] + _s5_const_specs(1),
        out_specs=[pl.BlockSpec((bs, S5_W), lambda i: (0, 0)),
                   pl.BlockSpec((bs, S5_S), lambda i: (0, 0)),
                   pl.BlockSpec((bs, S5_S), lambda i: (0, 0))],
        out_shape=[jax.ShapeDtypeStruct((bs, S5_W), BF16),
                   jax.ShapeDtypeStruct((bs, S5_S), F32),
                   jax.ShapeDtypeStruct((bs, S5_S), F32)],
        compiler_params=_cparams(("arbitrary",)),
        name="s5_sample",
    )(proj, s0_re, s0_im, *consts)


ATT_TQ = 384
ATT_HG = 2
LOG2E = 1.4426950408889634


def _fox_prompt_kernel(q_ref, k_ref, v_ref, c_ref, o_ref, kb, vb):
    qi = pl.program_id(2)
    t = ATT_TQ

    @pl.when(qi == 0)
    def _():
        kb[...] = k_ref[...].astype(BF16)
        vb[...] = v_ref[...].astype(BF16)

    k2 = (ATT_D ** -0.5) * LOG2E
    qs = [(q_ref[:, h * ATT_D:(h + 1) * ATT_D] * k2).astype(BF16) for h in range(ATT_HG)]

    def block(j, carry, masked):
        start = pl.multiple_of(j * t, 128)
        out = []
        for h in range(ATT_HG):
            m, l, acc = carry[3 * h:3 * h + 3]
            hs = slice(h * ATT_D, (h + 1) * ATT_D)
            s = lax.dot_general(qs[h], kb[pl.ds(start, t), hs], (((1,), (1,)), ((), ())),
                                preferred_element_type=F32)
            s = s - c_ref[h:h + 1, pl.ds(start, t)] * LOG2E
            if masked:
                row = lax.broadcasted_iota(jnp.int32, (t, t), 0)
                col = lax.broadcasted_iota(jnp.int32, (t, t), 1)
                s = jnp.where(row >= col, s, -jnp.inf)
            m_new = jnp.maximum(m, jnp.max(s, axis=-1, keepdims=True))
            alpha = jnp.exp2(m - m_new)
            p = jnp.exp2(s - m_new)
            l = alpha * l + jnp.sum(p, axis=-1, keepdims=True)
            acc = alpha * acc + jnp.dot(p.astype(BF16), vb[pl.ds(start, t), hs], preferred_element_type=F32)
            out += [m_new, l, acc]
        return tuple(out)

    init = (jnp.full((t, 1), -jnp.inf, F32), jnp.zeros((t, 1), F32), jnp.zeros((t, ATT_D), F32)) * ATT_HG
    carry = lax.fori_loop(0, qi, lambda j, cr: block(j, cr, False), init)
    res = block(qi, carry, True)
    for h in range(ATT_HG):
        o_ref[:, h * ATT_D:(h + 1) * ATT_D] = (res[3 * h + 2] / res[3 * h + 1]).astype(BF16)


def _fox_prompt(proj, catt, *, bp, lp):
    nq = lp // ATT_TQ
    w = ATT_HG * ATT_D
    return pl.pallas_call(
        _fox_prompt_kernel,
        grid=(bp, ATT_H // ATT_HG, nq),
        in_specs=[pl.BlockSpec((ATT_TQ, w), lambda b, h, i: (b * nq + i, COL_Q // w + h)),
                  pl.BlockSpec((lp, w), lambda b, h, i: (b, COL_K // w + h)),
                  pl.BlockSpec((lp, w), lambda b, h, i: (b, COL_V // w + h)),
                  pl.BlockSpec((None, None, ATT_HG, lp), lambda b, h, i: (b, h, 0, 0))],
        out_specs=pl.BlockSpec((ATT_TQ, w), lambda b, h, i: (b * nq + i, h)),
        out_shape=jax.ShapeDtypeStruct((bp * lp, ATT_W), BF16),
        scratch_shapes=[pltpu.VMEM((lp, w), BF16), pltpu.VMEM((lp, w), BF16)],
        compiler_params=_cparams(("parallel", "parallel", "arbitrary")),
        name="fox_prompt",
    )(proj, proj, proj, catt.reshape(bp, ATT_H // ATT_HG, ATT_HG, lp))


def _fox_sample_kernel(pt_ref, q_ref, k_ref, v_ref, lfn_ref, ck_hbm, cv_hbm, cl_hbm, o_ref,
                       kbuf, vbuf, lbuf, lg, sems, *, n_seq, n_pages, page0):
    b = pl.program_id(0)
    slot = b % 2
    flat = PAGE * ATT_H

    def copies(seq, sl):
        out = []
        for j in range(n_pages):
            page = pt_ref[seq, j] + page0
            out.append(pltpu.make_async_copy(ck_hbm.at[page], kbuf.at[sl, j], sems.at[sl, 0]))
            out.append(pltpu.make_async_copy(cv_hbm.at[page], vbuf.at[sl, j], sems.at[sl, 1]))
            out.append(pltpu.make_async_copy(cl_hbm.at[pl.ds(page, 1)], lbuf.at[sl, pl.ds(j, 1)], sems.at[sl, 2]))
        return out

    @pl.when(b == 0)
    def _():
        for cp in copies(b, slot):
            cp.start()

    @pl.when(b + 1 < n_seq)
    def _():
        for cp in copies(b + 1, 1 - slot):
            cp.start()

    for cp in copies(b, slot):
        cp.wait()

    q = q_ref[...]
    qb = q.astype(BF16)
    scale = ATT_D ** -0.5
    lf = lbuf[slot]
    lane = lax.broadcasted_iota(jnp.int32, (n_pages, flat), 1)
    incl = lf
    tot = lf
    sh = ATT_H
    while sh < flat:
        incl = incl + jnp.where(lane + sh < flat, pltpu.roll(incl, flat - sh, 1), 0.0)
        tot = tot + pltpu.roll(tot, sh, 1)
        sh *= 2
    pj = lax.broadcasted_iota(jnp.int32, (n_pages, n_pages), 0)
    pk = lax.broadcasted_iota(jnp.int32, (n_pages, n_pages), 1)
    later = jnp.dot((pk > pj).astype(F32), tot, precision=HIGHEST, preferred_element_type=F32)
    bias = incl - lf + later

    hrow = lax.broadcasted_iota(jnp.int32, (ATT_H, flat), 0)
    hlane = lax.broadcasted_iota(jnp.int32, (ATT_H, flat), 1)
    own = (hlane % ATT_H) == hrow
    m = jnp.sum(q * k_ref[...], axis=-1, keepdims=True) * scale
    s_self = m
    for j in range(n_pages):
        s = lax.dot_general(qb, kbuf[slot, j].astype(BF16), (((1,), (1,)), ((), ())),
                            preferred_element_type=F32)
        s = jnp.where(own, s * scale + bias[j:j + 1, :] + lfn_ref[...], -jnp.inf)
        lg[j] = s
        m = jnp.maximum(m, jnp.max(s, axis=-1, keepdims=True))
    p_self = jnp.exp(s_self - m)
    l = p_self
    acc = p_self * v_ref[...]
    for j in range(n_pages):
        p = jnp.exp(lg[j] - m)
        l = l + jnp.sum(p, axis=-1, keepdims=True)
        acc = acc + jnp.dot(p.astype(BF16), vbuf[slot, j].astype(BF16), preferred_element_type=F32)
    o_ref[...] = acc / l


def _fox_sample(page_table, q3, k3, v3, lfn3, ck, cv, cl, page0):
    n_seq, n_pages = page_table.shape
    flat = PAGE * ATT_H
    kern = functools.partial(_fox_sample_kernel, n_seq=n_seq, n_pages=n_pages, page0=page0)
    seq_spec = pl.BlockSpec((None, ATT_H, ATT_D), lambda b, pt: (b, 0, 0))
    return pl.pallas_call(
        kern,
        grid_spec=pltpu.PrefetchScalarGridSpec(
            num_scalar_prefetch=1,
            grid=(n_seq,),
            in_specs=[seq_spec, seq_spec, seq_spec, pl.BlockSpec((None, ATT_H, 1), lambda b, pt: (b, 0, 0)),
                      pl.BlockSpec(memory_space=pl.ANY), pl.BlockSpec(memory_space=pl.ANY),
                      pl.BlockSpec(memory_space=pl.ANY)],
            out_specs=pl.BlockSpec((None, ATT_H, ATT_D), lambda b, pt: (b, 0, 0)),
            scratch_shapes=[pltpu.VMEM((2, n_pages, flat, ATT_D), F32),
                            pltpu.VMEM((2, n_pages, flat, ATT_D), F32),
                            pltpu.VMEM((2, n_pages, flat), F32),
                            pltpu.VMEM((n_pages, ATT_H, flat), F32),
                            pltpu.SemaphoreType.DMA((2, 3))]),
        out_shape=jax.ShapeDtypeStruct((n_seq, ATT_H, ATT_D), F32),
        compiler_params=_cparams(("arbitrary",)),
        name="fox_sample",
    )(page_table, q3, k3, v3, lfn3, ck, cv, cl)


def _outproj_kernel(h_ref, ya_ref, yb_ref, yc_ref, sa_ref, sb_ref, sc_ref, w_ref, o_ref, *, n_prompt_tiles):
    i = pl.program_id(0)

    def mix(a, b, c):
        return (jnp.dot(a, w_ref[0:SSD_W, :], preferred_element_type=F32)
                + jnp.dot(b, w_ref[SSD_W:SSD_W + S5_W, :], preferred_element_type=F32)
                + jnp.dot(c, w_ref[SSD_W + S5_W:, :], preferred_element_type=F32))

    @pl.when(i < n_prompt_tiles)
    def _():
        o_ref[...] = h_ref[...] + mix(ya_ref[...], yb_ref[...], yc_ref[...])

    @pl.when(i >= n_prompt_tiles)
    def _():
        bs = sa_ref.shape[0]
        o_ref[0:bs, :] = h_ref[0:bs, :] + mix(sa_ref[...], sb_ref[...], sc_ref[...])
        o_ref[bs:, :] = h_ref[bs:, :]


def _outproj(h, ya, yb, yc, sa, sb, sc, w, layer, tm=256):
    r, d = h.shape
    npt = ya.shape[0] // tm
    bs = sa.shape[0]
    assert ya.shape[0] % tm == 0 and r == (npt + 1) * tm and bs <= tm
    kern = functools.partial(_outproj_kernel, n_prompt_tiles=npt)
    pidx = lambda i: (jnp.minimum(i, npt - 1), 0)
    return pl.pallas_call(
        kern,
        grid=(npt + 1,),
        in_specs=[pl.BlockSpec((tm, d), lambda i: (i, 0)),
                  pl.BlockSpec((tm, SSD_W), pidx), pl.BlockSpec((tm, S5_W), pidx), pl.BlockSpec((tm, ATT_W), pidx),
                  pl.BlockSpec((bs, SSD_W), lambda i: (0, 0)), pl.BlockSpec((bs, S5_W), lambda i: (0, 0)),
                  pl.BlockSpec((bs, ATT_W), lambda i: (0, 0)),
                  pl.BlockSpec((None, d, d), lambda i: (layer, 0, 0))],
        out_specs=pl.BlockSpec((tm, d), lambda i: (i, 0)),
        out_shape=jax.ShapeDtypeStruct((r, d), F32),
        compiler_params=_cparams(("parallel",)),
        name="outproj",
    )(h, ya, yb, yc, sa, sb, sc, w)


def _block_diag(blocks):
    g, a, b = blocks.shape
    eye = jnp.eye(g, dtype=blocks.dtype)
    return (eye[:, None, :, None] * blocks[:, :, None, :]).reshape(g * a, g * b)


def kernel(x_prompt, x_sample, cache_k, cache_v, cache_logf, state_ssd, state_ssd_conv, state_s5_re, state_s5_im,
           page_table, meta_tokens, ffn1_norm, ffn1_gate, ffn1_up, ffn1_down, mix_norm, w_in, ssd_conv_w,
           ssd_conv_b, ssd_dt_bias, ssd_A_log, ssd_D, ssd_norm, s5_A_re, s5_A_im, s5_log_step, s5_B_re, s5_B_im,
           s5_C_re, s5_C_im, s5_D, s5_w_glu, s5_b_glu, att_q_norm, att_k_norm, att_f_bias, w_out, ffn2_norm,
           ffn2_gate, ffn2_up, ffn2_down):
    bp, seq, d = x_prompt.shape
    bs = x_sample.shape[0]
    depth = w_in.shape[0]
    n_phys = cache_k.shape[1]
    l_real = seq + N_META
    lp = -(-l_real // (3 * CHUNK)) * (3 * CHUNK)
    row_s = bp * lp
    r_rows = row_s + 2 * bs
    assert lp % ATT_TQ == 0 and row_s % 256 == 0 and r_rows % 512 == 0 and bs == 128

    meta = meta_tokens.astype(F32)
    pad = jnp.zeros((lp - l_real, d), F32)
    pieces = []
    for b in range(bp):
        pieces += [meta, x_prompt[b], pad]
    pieces += [x_sample[:, 0, :], jnp.zeros((bs, d), F32)]
    h = jnp.concatenate(pieces, axis=0)

    expand = (jnp.arange(128)[:, None] == (jnp.arange(SSD_W)[None, :] // SSD_P)).astype(F32)
    n_tiles = PROJ_W // PROJ_TN
    ck_all = cache_k.reshape(depth * n_phys, PAGE * ATT_H, ATT_D)
    cv_all = cache_v.reshape(depth * n_phys, PAGE * ATT_H, ATT_D)
    cl_all = cache_logf.reshape(depth * n_phys, PAGE * ATT_H)

    w1g, w1u, w1d = ffn1_gate.astype(BF16), ffn1_up.astype(BF16), ffn1_down.astype(BF16)
    w2g, w2u, w2d = ffn2_gate.astype(BF16), ffn2_up.astype(BF16), ffn2_down.astype(BF16)
    w_out_all = w_out.astype(BF16)
    z_w, xbc_w, dt_w, u_w, q_w, k_w, v_w, f_w = jnp.split(w_in, [512, 1536, 1544, 2056, 3080, 4104, 5128], axis=2)
    w_proj_all = jnp.concatenate(
        [z_w, xbc_w, u_w, q_w, k_w, v_w, dt_w, f_w,
         jnp.zeros((depth, d, PROJ_W - COL_DTF - 2 * SSD_HEADS), F32)], axis=2).astype(BF16)

    outs = {k: [] for k in ("k_p", "v_p", "lf_p", "k_s", "v_s", "lf_s", "ssd_p", "ssd_s", "conv_p", "conv_s",
                            "re_p", "re_s", "im_p", "im_s")}
    for i in range(depth):
        gains = jnp.ones((n_tiles, 1, PROJ_TN), F32)
        gq = jnp.tile(att_q_norm[i], PROJ_TN // ATT_D)
        gk = jnp.tile(att_k_norm[i], PROJ_TN // ATT_D)
        gains = gains.at[COL_Q // PROJ_TN:COL_K // PROJ_TN, 0].set(gq).at[COL_K // PROJ_TN:COL_V // PROJ_TN, 0].set(gk)
        bias = jnp.zeros((1, PROJ_TN), F32).at[0, 0:8].set(ssd_dt_bias[i]).at[0, 8:16].set(att_f_bias[i])
        cw, cb = ssd_conv_w[i], ssd_conv_b[i][None, :]
        alog_row = jnp.zeros((1, 128), F32).at[0, 0:SSD_HEADS].set(ssd_A_log[i])
        aexp = jnp.repeat(ssd_A_log[i], SSD_P)[None, :]
        dexp = jnp.repeat(ssd_D[i], SSD_P)[None, :]
        gn = ssd_norm[i][None, :]

        h = _ffn(h, ffn1_norm[i][None, :], w1g, w1u, w1d, i)
        proj = _inproj(h, mix_norm[i][None, :], w_proj_all, gains, bias, i)

        y_ssd_p, catt, ssd_p, conv_p = _ssd_prompt(proj, cw, cb, alog_row, dexp, gn, bp=bp, lp=lp, l_real=l_real)
        y_ssd_s, conv_s, ssd_s = _ssd_sample(proj, jnp.moveaxis(state_ssd_conv[i], 1, 0), state_ssd,
                                             cw, cb, aexp, dexp, gn, expand, row0=row_s, bs=bs, layer=i)
        lam_re, lam_im, pw_re, pw_im, bb_re, bb_im = _s5_params(
            s5_A_re[i], s5_A_im[i], s5_log_step[i][:, None],
            jnp.swapaxes(s5_B_re[i], 1, 2), jnp.swapaxes(s5_B_im[i], 1, 2))
        consts = (_block_diag(bb_re).astype(BF16), _block_diag(bb_im).astype(BF16),
                  pw_re.reshape(8, S5_S), pw_im.reshape(8, S5_S),
                  _block_diag(jnp.swapaxes(s5_C_re[i], 1, 2)).astype(BF16),
                  _block_diag(jnp.swapaxes(s5_C_im[i], 1, 2)).astype(BF16),
                  s5_D[i].reshape(1, S5_W), s5_w_glu[i].astype(BF16), s5_b_glu[i][None, :])
        y_s5_p, re_p, im_p = _s5_prompt(proj, consts, bp=bp, lp=lp, l_real=l_real)
        y_s5_s, re_s, im_s = _s5_sample(proj, state_s5_re[i].reshape(bs, S5_S), state_s5_im[i].reshape(bs, S5_S),
                                        consts, row0=row_s, bs=bs)
        y_att_p = _fox_prompt(proj, catt, bp=bp, lp=lp)
        samp = proj[row_s:row_s + bs]
        q3 = samp[:, COL_Q:COL_K].reshape(bs, ATT_H, ATT_D)
        k3 = samp[:, COL_K:COL_V].reshape(bs, ATT_H, ATT_D)
        v3 = samp[:, COL_V:COL_DTF].reshape(bs, ATT_H, ATT_D)
        lfn3 = samp[:, COL_DTF + 8:COL_DTF + 16].reshape(bs, ATT_H, 1)
        y_att_s = _fox_sample(page_table, q3, k3, v3, lfn3,
                              ck_all, cv_all, cl_all, i * n_phys)
        h = _outproj(h, y_ssd_p, y_s5_p, y_att_p, y_ssd_s, y_s5_s,
                     y_att_s.reshape(bs, ATT_W).astype(BF16), w_out_all, i)
        h = _ffn(h, ffn2_norm[i][None, :], w2g, w2u, w2d, i)

        def prompt_cols(c0, c1):
            return jnp.stack([proj[b * lp:b * lp + l_real, c0:c1] for b in range(bp)], axis=0)

        outs["k_p"].append(prompt_cols(COL_K, COL_V).reshape(bp, l_real, ATT_H, ATT_D))
        outs["v_p"].append(prompt_cols(COL_V, COL_DTF).reshape(bp, l_real, ATT_H, ATT_D))
        outs["lf_p"].append(prompt_cols(COL_DTF + 8, COL_DTF + 16))
        outs["k_s"].append(k3[:, None])
        outs["v_s"].append(v3[:, None])
        outs["lf_s"].append(samp[:, None, COL_DTF + 8:COL_DTF + 16])
        outs["ssd_p"].append(jnp.swapaxes(ssd_p, 2, 3))
        outs["ssd_s"].append(ssd_s)
        outs["conv_p"].append(conv_p)
        outs["conv_s"].append(jnp.moveaxis(conv_s, 0, 1))
        outs["re_p"].append(re_p.reshape(bp, S5_G, S5_N))
        outs["re_s"].append(re_s.reshape(bs, S5_G, S5_N))
        outs["im_p"].append(im_p.reshape(bp, S5_G, S5_N))
        outs["im_s"].append(im_s.reshape(bs, S5_G, S5_N))

    st = {k: jnp.stack(v, axis=0) for k, v in outs.items()}
    y_prompt = jnp.stack([h[b * lp + N_META:b * lp + l_real] for b in range(bp)], axis=0)
    y_sample = h[row_s:row_s + bs][:, None, :]
    return (y_prompt, y_sample, st["k_p"], st["v_p"], st["lf_p"], st["k_s"], st["v_s"], st["lf_s"],
            st["ssd_p"], st["ssd_s"], st["conv_p"], st["conv_s"], st["re_p"], st["re_s"], st["im_p"], st["im_s"])
```

```python
import functools
import math

import jax
import jax.numpy as jnp
from jax import lax
from jax.experimental import pallas as pl
from jax.experimental.pallas import tpu as pltpu

F32 = jnp.float32
BF16 = jnp.bfloat16
HIGHEST = lax.Precision.HIGHEST

NORM_EPS = 1e-6
N_META = 16
PAGE = 128
CHUNK = 128
SSD_HEADS, SSD_P, SSD_N, SSD_G = 8, 64, 128, 2
SSD_W = SSD_HEADS * SSD_P
S5_G, S5_C, S5_N = 32, 16, 64
S5_W = S5_G * S5_C
S5_S = S5_G * S5_N
ATT_H, ATT_D = 8, 128
ATT_W = ATT_H * ATT_D
PROJ_TN = 512
COL_Z, COL_X, COL_BC, COL_U, COL_Q, COL_K, COL_V, COL_DTF = 0, 512, 1024, 1536, 2048, 3072, 4096, 5120
PROJ_W = 5632
VMEM_LIMIT = 56 * 1024 * 1024


def _cparams(sem):
    return pltpu.CompilerParams(dimension_semantics=sem, vmem_limit_bytes=VMEM_LIMIT)


def _rms(x, g):
    return x * lax.rsqrt(jnp.mean(x * x, axis=-1, keepdims=True) + NORM_EPS) * g


def _silu(x):
    return x * jax.nn.sigmoid(x)


def _softplus(x):
    return jnp.maximum(x, 0.0) + jnp.log1p(jnp.exp(-jnp.abs(x)))


def _ffn_kernel(x_ref, g_ref, wg_ref, wu_ref, wd_ref, o_ref, xn_ref):
    j = pl.program_id(1)

    @pl.when(j == 0)
    def _():
        x = x_ref[...]
        xn_ref[...] = _rms(x, g_ref[...]).astype(BF16)
        o_ref[...] = x

    xn = xn_ref[...]
    gate = jnp.dot(xn, wg_ref[...], preferred_element_type=F32)
    up = jnp.dot(xn, wu_ref[...], preferred_element_type=F32)
    act = (0.5 * _silu(gate) * up).astype(BF16)
    o_ref[...] += jnp.dot(act, wd_ref[...], preferred_element_type=F32)


def _ffn(h, g, wg, wu, wd, layer, tm=512, tf=512):
    r, d = h.shape
    f = wg.shape[2]
    return pl.pallas_call(
        _ffn_kernel,
        grid=(r // tm, f // tf),
        in_specs=[pl.BlockSpec((tm, d), lambda i, j: (i, 0)),
                  pl.BlockSpec((1, d), lambda i, j: (0, 0)),
                  pl.BlockSpec((None, d, tf), lambda i, j: (layer, 0, j)),
                  pl.BlockSpec((None, d, tf), lambda i, j: (layer, 0, j)),
                  pl.BlockSpec((None, tf, d), lambda i, j: (layer, j, 0))],
        out_specs=pl.BlockSpec((tm, d), lambda i, j: (i, 0)),
        out_shape=jax.ShapeDtypeStruct((r, d), F32),
        scratch_shapes=[pltpu.VMEM((tm, d), BF16)],
        compiler_params=_cparams(("parallel", "arbitrary")),
        name="ffn",
    )(h, g, wg, wu, wd)


def _inproj_kernel(x_ref, g_ref, w_ref, gain_ref, bias_ref, o_ref, xn_ref):
    j = pl.program_id(1)

    @pl.when(j == 0)
    def _():
        xn_ref[...] = _rms(x_ref[...], g_ref[...]).astype(BF16)

    acc = jnp.dot(xn_ref[...], w_ref[...], preferred_element_type=F32)
    jq, jv, jd = COL_Q // PROJ_TN, COL_V // PROJ_TN, COL_DTF // PROJ_TN
    is_qk = jnp.logical_and(j >= jq, j < jv)

    @pl.when(jnp.logical_and(jnp.logical_not(is_qk), j != jd))
    def _():
        o_ref[...] = acc

    @pl.when(is_qk)
    def _():
        for hh in range(PROJ_TN // ATT_D):
            sl = slice(hh * ATT_D, (hh + 1) * ATT_D)
            o_ref[:, sl] = _rms(acc[:, sl], gain_ref[:, sl])

    @pl.when(j == jd)
    def _():
        a = acc[:, :128] + bias_ref[:, :128]
        lane = lax.broadcasted_iota(jnp.int32, a.shape, 1)
        is_dt = lane < SSD_HEADS
        sp = _softplus(jnp.where(is_dt, a, -a))
        o_ref[:, :128] = jnp.where(is_dt, sp, -sp)
        o_ref[:, 128:] = jnp.zeros((acc.shape[0], PROJ_TN - 128), F32)


def _inproj(h, g, w, gains, bias, layer, tm=1088):
    r, d = h.shape
    n = w.shape[2]
    tn = PROJ_TN
    return pl.pallas_call(
        _inproj_kernel,
        grid=(r // tm, n // tn),
        in_specs=[pl.BlockSpec((tm, d), lambda i, j: (i, 0)),
                  pl.BlockSpec((1, d), lambda i, j: (0, 0)),
                  pl.BlockSpec((None, d, tn), lambda i, j: (layer, 0, j)),
                  pl.BlockSpec((None, 1, tn), lambda i, j: (j, 0, 0)),
                  pl.BlockSpec((1, tn), lambda i, j: (0, 0))],
        out_specs=pl.BlockSpec((tm, tn), lambda i, j: (i, j)),
        out_shape=jax.ShapeDtypeStruct((r, n), F32),
        scratch_shapes=[pltpu.VMEM((tm, d), BF16)],
        compiler_params=_cparams(("parallel", "arbitrary")),
        name="inproj",
    )(h, g, w, gains, bias)


def _ssd_gate_norm(y, z, gn):
    yz = y * _silu(z)
    gw = SSD_W // SSD_G
    outs = []
    for g in range(SSD_G):
        blk = yz[:, g * gw:(g + 1) * gw]
        outs.append(blk * lax.rsqrt(jnp.mean(blk * blk, axis=-1, keepdims=True) + NORM_EPS)
                    * gn[:, g * gw:(g + 1) * gw])
    return outs


def _ssd_prompt_kernel(z_ref, xx_ref, bc_ref, dtf_ref, k_ref, v_ref, cw_ref, cb_ref, alog_ref, dexp_ref, gn_ref,
                       y_ref, catt_ref, st_ref, cst_ref, kst_ref, vst_ref, lf_ref, xbuf, hst, ccarry, ybuf,
                       *, l_real, n_chunks):
    c = pl.program_id(1)
    q = CHUNK
    kst_ref[...] = k_ref[...]
    vst_ref[...] = v_ref[...]
    lf_ref[...] = dtf_ref[:, SSD_HEADS:2 * SSD_HEADS]

    @pl.when(c == 0)
    def _():
        xbuf[0:8, :] = jnp.zeros((8, xbuf.shape[1]), F32)
        hst[...] = jnp.zeros(hst.shape, F32)
        ccarry[...] = jnp.zeros(ccarry.shape, F32)

    xbuf[8:8 + q, 0:SSD_W] = xx_ref[...]
    xbuf[8:8 + q, SSD_W:] = bc_ref[...]
    conv = cb_ref[...] + cw_ref[0:1, :] * xbuf[5:5 + q, :]
    for k in range(1, 4):
        conv = conv + cw_ref[k:k + 1, :] * xbuf[5 + k:5 + k + q, :]
    xc = _silu(conv)

    @pl.when(c == n_chunks - 1)
    def _():
        r0 = 8 + l_real - q * (n_chunks - 1) - 3
        cst_ref[...] = xbuf[r0:r0 + 3, :]

    xbuf[0:8, :] = xbuf[q:q + 8, :]

    row = lax.broadcasted_iota(jnp.int32, (q, q), 0)
    col = lax.broadcasted_iota(jnp.int32, (q, q), 1)
    lane1 = lax.broadcasted_iota(jnp.int32, (1, q), 1)
    valid = (c * q + row) < l_real
    dtm = jnp.where(valid, dtf_ref[...], 0.0)
    a_row = -jnp.exp(alog_ref[...])
    scale_row = jnp.where(lane1 < SSD_HEADS, a_row, jnp.where(lane1 < 2 * SSD_HEADS, 1.0, 0.0))
    causal = row >= col
    cs = jnp.dot(causal.astype(F32), dtm * scale_row, precision=HIGHEST, preferred_element_type=F32)
    cs_t = cs.T
    catt_ref[...] = (cs + ccarry[0:1, :]).T[SSD_HEADS:2 * SSD_HEADS, :]
    ccarry[0:1, :] = ccarry[0:1, :] + cs[q - 1:q, :]

    hpg = SSD_HEADS // SSD_G
    for g in range(SSD_G):
        bg = xc[:, SSD_W + g * SSD_N:SSD_W + (g + 1) * SSD_N]
        cg = xc[:, SSD_W + (SSD_G + g) * SSD_N:SSD_W + (SSD_G + g + 1) * SSD_N].astype(BF16)
        bg_t = bg.T.astype(BF16)
        cb = jnp.dot(cg, bg_t, preferred_element_type=F32)
        for hh in range(hpg):
            h = g * hpg + hh
            xh = xc[:, h * SSD_P:(h + 1) * SSD_P]
            acs_col = cs[:, h:h + 1]
            seg = acs_col - cs_t[h:h + 1, :]
            decay = jnp.exp(jnp.where(causal, seg, -jnp.inf))
            xdt = xh * dtm[:, h:h + 1]
            y = jnp.dot((cb * decay).astype(BF16), xdt.astype(BF16), preferred_element_type=F32)
            h_t = hst[h]
            y = y + jnp.dot(cg, h_t.astype(BF16), preferred_element_type=F32) * jnp.exp(acs_col)
            a_last = cs[q - 1:q, h:h + 1]
            xw = (xdt * jnp.exp(a_last - acs_col)).astype(BF16)
            hst[h] = h_t * jnp.exp(a_last) + jnp.dot(bg_t, xw, preferred_element_type=F32)
            ybuf[:, h * SSD_P:(h + 1) * SSD_P] = y + dexp_ref[:, h * SSD_P:(h + 1) * SSD_P] * xh

    outs = _ssd_gate_norm(ybuf[...], z_ref[...], gn_ref[...])
    gw = SSD_W // SSD_G
    for g in range(SSD_G):
        y_ref[:, g * gw:(g + 1) * gw] = outs[g].astype(BF16)

    @pl.when(c == n_chunks - 1)
    def _():
        st_ref[...] = hst[...]


def _ssd_prompt(proj, cw, cb, alog_row, dexp, gn, *, bp, lp, l_real):
    nc = lp // CHUNK
    kern = functools.partial(_ssd_prompt_kernel, l_real=l_real, n_chunks=nc)
    rowblk = lambda b, c: b * nc + c
    return pl.pallas_call(
        kern,
        grid=(bp, nc),
        in_specs=[pl.BlockSpec((CHUNK, 512), lambda b, c: (rowblk(b, c), COL_Z // 512)),
                  pl.BlockSpec((CHUNK, 512), lambda b, c: (rowblk(b, c), COL_X // 512)),
                  pl.BlockSpec((CHUNK, 512), lambda b, c: (rowblk(b, c), COL_BC // 512)),
                  pl.BlockSpec((CHUNK, 128), lambda b, c: (rowblk(b, c), COL_DTF // 128)),
                  pl.BlockSpec((CHUNK, ATT_W), lambda b, c: (rowblk(b, c), COL_K // ATT_W)),
                  pl.BlockSpec((CHUNK, ATT_W), lambda b, c: (rowblk(b, c), COL_V // ATT_W)),
                  pl.BlockSpec((4, 1024), lambda b, c: (0, 0)),
                  pl.BlockSpec((1, 1024), lambda b, c: (0, 0)),
                  pl.BlockSpec((1, 128), lambda b, c: (0, 0)),
                  pl.BlockSpec((1, 512), lambda b, c: (0, 0)),
                  pl.BlockSpec((1, 512), lambda b, c: (0, 0))],
        out_specs=[pl.BlockSpec((CHUNK, SSD_W), lambda b, c: (rowblk(b, c), 0)),
                   pl.BlockSpec((None, SSD_HEADS, CHUNK), lambda b, c: (b, 0, c)),
                   pl.BlockSpec((None, SSD_HEADS, SSD_N, SSD_P), lambda b, c: (b, 0, 0, 0)),
                   pl.BlockSpec((None, 3, 1024), lambda b, c: (b, 0, 0)),
                   pl.BlockSpec((None, CHUNK, ATT_W), lambda b, c: (b, c, 0)),
                   pl.BlockSpec((None, CHUNK, ATT_W), lambda b, c: (b, c, 0)),
                   pl.BlockSpec((None, CHUNK, ATT_H), lambda b, c: (b, c, 0))],
        out_shape=[jax.ShapeDtypeStruct((bp * lp, SSD_W), BF16),
                   jax.ShapeDtypeStruct((bp, ATT_H, lp), F32),
                   jax.ShapeDtypeStruct((bp, SSD_HEADS, SSD_N, SSD_P), F32),
                   jax.ShapeDtypeStruct((bp, 3, 1024), F32),
                   jax.ShapeDtypeStruct((bp, lp, ATT_W), F32),
                   jax.ShapeDtypeStruct((bp, lp, ATT_W), F32),
                   jax.ShapeDtypeStruct((bp, lp, ATT_H), F32)],
        scratch_shapes=[pltpu.VMEM((CHUNK + 8, 1024), F32),
                        pltpu.VMEM((SSD_HEADS, SSD_N, SSD_P), F32),
                        pltpu.VMEM((8, 128), F32),
                        pltpu.VMEM((CHUNK, SSD_W), F32)],
        compiler_params=_cparams(("parallel", "arbitrary")),
        name="ssd_prompt",
    )(proj, proj, proj, proj, proj, proj, cw, cb, alog_row, dexp, gn)


SSD_SB = 16


def _ssd_sample_kernel(z_ref, xx_ref, bc_ref, dtf_ref, cst_ref, st_ref, cw_ref, cb_ref, aexp_ref, dexp_ref,
                       gn_ref, expand_ref, y_ref, cst_out, st_out, dparts, xdt_t, bm_s, cmb, y_t, *, n_steps):
    i = pl.program_id(0)
    bs = z_ref.shape[0]

    @pl.when(i == 0)
    def _():
        w = cw_ref[...]
        for half, ref in ((0, xx_ref), (1, bc_ref)):
            sl = slice(half * 512, (half + 1) * 512)
            xn = ref[...]
            conv = (cb_ref[:, sl] + w[0:1, sl] * cst_ref[0, :, sl] + w[1:2, sl] * cst_ref[1, :, sl]
                    + w[2:3, sl] * cst_ref[2, :, sl] + w[3:4, sl] * xn)
            xcv = _silu(conv)
            cst_out[0, :, sl] = cst_ref[1, :, sl]
            cst_out[1, :, sl] = cst_ref[2, :, sl]
            cst_out[2, :, sl] = xn
            if half == 0:
                dt_e = jnp.dot(dtf_ref[...], expand_ref[...], precision=HIGHEST, preferred_element_type=F32)
                dec = jnp.exp(dt_e * (-jnp.exp(aexp_ref[...]))).T
                hi = dec.astype(BF16).astype(F32)
                mid = (dec - hi).astype(BF16).astype(F32)
                dparts[0] = hi
                dparts[1] = mid
                dparts[2] = dec - hi - mid
                xdt_t[...] = (xcv * dt_e).T
                y_t[...] = (dexp_ref[...] * xcv).T
            else:
                bm_s[...] = xcv[:, 0:SSD_G * SSD_N]
                cmb[...] = xcv[:, SSD_G * SSD_N:].astype(BF16)

    lane = lax.broadcasted_iota(jnp.int32, (1, bs), 1)
    shift = (bs - i * SSD_SB) % bs
    lhs = jnp.where(lane < SSD_SB, pltpu.roll(xdt_t[...], shift, 1), 0.0)
    for k in range(3):
        lo_l = (k + 1) * SSD_SB
        piece = pltpu.roll(dparts[k], (shift + lo_l) % bs, 1)
        lhs = jnp.where(jnp.logical_and(lane >= lo_l, lane < lo_l + SSD_SB), piece, lhs)
    lhs = lhs.astype(BF16)
    rowi = lax.broadcasted_iota(jnp.int32, (bs, SSD_N), 0)
    krow = rowi & (SSD_SB - 1)
    decf = jnp.where(jnp.logical_and(rowi >= SSD_SB, rowi < 4 * SSD_SB), 1.0, 0.0)
    hpg = SSD_HEADS // SSD_G
    gr = hpg * SSD_P
    yacc = [jnp.zeros((gr, bs), F32) for _ in range(SSD_G)]
    bm_blk = bm_s[pl.ds(pl.multiple_of(i * SSD_SB, SSD_SB), SSD_SB), :]
    for bb in range(SSD_SB):
        seq = i * SSD_SB + bb
        right = jnp.where(krow == bb, decf, 0.0).astype(BF16)
        is_bb = rowi == bb
        onehot = (lane == seq).astype(F32)
        for g in range(SSD_G):
            brow = bm_blk[bb:bb + 1, g * SSD_N:(g + 1) * SSD_N]
            rhs = jnp.concatenate([jnp.where(is_bb, brow, 0.0).astype(BF16), right], axis=1)
            out = jnp.dot(lhs[g * gr:(g + 1) * gr, :], rhs, preferred_element_type=F32)
            st = st_ref[bb, g * hpg:(g + 1) * hpg].reshape(gr, SSD_N)
            hn = st * out[:, SSD_N:] + out[:, :SSD_N]
            st_out[bb, g * hpg:(g + 1) * hpg] = hn.reshape(hpg, SSD_P, SSD_N)
            res = lax.dot_general(hn.astype(BF16), cmb[:, g * SSD_N:(g + 1) * SSD_N], (((1,), (1,)), ((), ())),
                                  preferred_element_type=F32)
            yacc[g] = yacc[g] + res * onehot
    for g in range(SSD_G):
        y_t[g * gr:(g + 1) * gr, :] = y_t[g * gr:(g + 1) * gr, :] + yacc[g]

    @pl.when(i == n_steps - 1)
    def _():
        outs = _ssd_gate_norm(y_t[...].T, z_ref[...], gn_ref[...])
        gw = SSD_W // SSD_G
        for g in range(SSD_G):
            y_ref[:, g * gw:(g + 1) * gw] = outs[g].astype(BF16)


def _ssd_sample(proj, cst_t, st, cw, cb, aexp, dexp, gn, expand, *, row0, bs, layer):
    n_steps = bs // SSD_SB
    rb = row0 // bs
    kern = functools.partial(_ssd_sample_kernel, n_steps=n_steps)
    full = lambda shape: pl.BlockSpec(shape, lambda i: (0,) * len(shape))
    return pl.pallas_call(
        kern,
        grid=(n_steps,),
        in_specs=[pl.BlockSpec((bs, 512), lambda i: (rb, COL_Z // 512)),
                  pl.BlockSpec((bs, 512), lambda i: (rb, COL_X // 512)),
                  pl.BlockSpec((bs, 512), lambda i: (rb, COL_BC // 512)),
                  pl.BlockSpec((bs, 128), lambda i: (rb, COL_DTF // 128)),
                  full((3, bs, 1024)),
                  pl.BlockSpec((None, SSD_SB, SSD_HEADS, SSD_P, SSD_N), lambda i: (layer, i, 0, 0, 0)),
                  full((4, 1024)), full((1, 1024)), full((1, 512)), full((1, 512)), full((1, 512)),
                  full((128, 512))],
        out_specs=[full((bs, SSD_W)),
                   full((3, bs, 1024)),
                   pl.BlockSpec((SSD_SB, SSD_HEADS, SSD_P, SSD_N), lambda i: (i, 0, 0, 0))],
        out_shape=[jax.ShapeDtypeStruct((bs, SSD_W), BF16),
                   jax.ShapeDtypeStruct((3, bs, 1024), F32),
                   jax.ShapeDtypeStruct(st.shape[1:], F32)],
        scratch_shapes=[pltpu.VMEM((3, SSD_W, bs), F32), pltpu.VMEM((SSD_W, bs), F32),
                        pltpu.VMEM((bs, SSD_G * SSD_N), F32), pltpu.VMEM((bs, SSD_G * SSD_N), BF16),
                        pltpu.VMEM((SSD_W, bs), F32)],
        compiler_params=_cparams(("arbitrary",)),
        name="ssd_sample",
    )(proj, proj, proj, proj, cst_t, st, cw, cb, aexp, dexp, gn, expand)


def _s5_param_kernel(are_ref, aim_ref, ls_ref, bre_ref, bim_ref,
                     lre_ref, lim_ref, pre_ref, pim_ref, bbre_ref, bbim_ref):
    a_re, a_im = are_ref[...], aim_ref[...]
    step = jnp.exp(ls_ref[...])
    mag = jnp.exp(a_re * step)
    lr = mag * jnp.cos(a_im * step)
    li = mag * jnp.sin(a_im * step)
    den = a_re * a_re + a_im * a_im
    zr = ((lr - 1.0) * a_re + li * a_im) / den
    zi = (li * a_re - (lr - 1.0) * a_im) / den
    lre_ref[...] = lr
    lim_ref[...] = li
    pr, pi = lr, li
    for k in range(8):
        pre_ref[k] = pr
        pim_ref[k] = pi
        pr, pi = pr * lr - pi * li, pr * li + pi * lr
    for g in range(S5_G):
        bbre_ref[g] = zr[g:g + 1, :] * bre_ref[g] - zi[g:g + 1, :] * bim_ref[g]
        bbim_ref[g] = zr[g:g + 1, :] * bim_ref[g] + zi[g:g + 1, :] * bre_ref[g]


def _s5_params(a_re, a_im, log_step, b_re_t, b_im_t):
    gn = (S5_G, S5_N)
    return pl.pallas_call(
        _s5_param_kernel,
        out_shape=[jax.ShapeDtypeStruct(gn, F32), jax.ShapeDtypeStruct(gn, F32),
                   jax.ShapeDtypeStruct((8,) + gn, F32), jax.ShapeDtypeStruct((8,) + gn, F32),
                   jax.ShapeDtypeStruct((S5_G, S5_C, S5_N), F32), jax.ShapeDtypeStruct((S5_G, S5_C, S5_N), F32)],
        name="s5_params",
    )(a_re, a_im, log_step, b_re_t, b_im_t)


def _gelu_tanh(x):
    return 0.5 * x * (1.0 + jnp.tanh(math.sqrt(2.0 / math.pi) * (x + 0.044715 * (x * x * x))))


S5_SLABS = 4


def _s5_in(ub, bd_ref, k):
    ki, ko = S5_W // S5_SLABS, S5_S // S5_SLABS
    return jnp.dot(ub[:, k * ki:(k + 1) * ki], bd_ref[k * ki:(k + 1) * ki, k * ko:(k + 1) * ko],
                   preferred_element_type=F32)


def _s5_tail(u, s_re, s_im, cdre_ref, cdim_ref, d_ref, wglu_ref, bglu_ref):
    ki, ko = S5_S // S5_SLABS, S5_W // S5_SLABS
    sr, si = s_re.astype(BF16), s_im.astype(BF16)
    ys = jnp.concatenate(
        [jnp.dot(sr[:, k * ki:(k + 1) * ki], cdre_ref[k * ki:(k + 1) * ki, k * ko:(k + 1) * ko],
                 preferred_element_type=F32)
         - jnp.dot(si[:, k * ki:(k + 1) * ki], cdim_ref[k * ki:(k + 1) * ki, k * ko:(k + 1) * ko],
                   preferred_element_type=F32) for k in range(S5_SLABS)], axis=1) + d_ref[...] * u
    g5 = _gelu_tanh(ys)
    gate = jnp.dot(g5.astype(BF16), wglu_ref[...], preferred_element_type=F32) + bglu_ref[...]
    return (g5 * jax.nn.sigmoid(gate)).astype(BF16)


def _s5_scan_kernel(u_ref, bdre_ref, bdim_ref, pre_ref, pim_ref, cdre_ref, cdim_ref, d_ref, wglu_ref, bglu_ref,
                    y_ref, sre_out, sim_out, sre, sim, car_re, car_im, *, l_real, n_chunks):
    c = pl.program_id(1)
    q = S5_CHUNK

    @pl.when(c == 0)
    def _():
        car_re[...] = jnp.zeros(car_re.shape, F32)
        car_im[...] = jnp.zeros(car_im.shape, F32)

    u = u_ref[...]
    ub = u.astype(BF16)
    ko = S5_S // S5_SLABS
    for k in range(S5_SLABS):
        sre[:, k * ko:(k + 1) * ko] = _s5_in(ub, bdre_ref, k)
        sim[:, k * ko:(k + 1) * ko] = _s5_in(ub, bdim_ref, k)
    pwr, pwi = pre_ref[...], pim_ref[...]
    row = lax.broadcasted_iota(jnp.int32, (8, S5_S), 0)
    steps = [(s, jnp.where(row >= s, pwr[s - 1:s, :], 0.0), jnp.where(row >= s, pwi[s - 1:s, :], 0.0))
             for s in (1, 2, 4)]

    def tile_body(t, carry):
        cr, ci = carry
        r0 = pl.multiple_of(t * 8, 8)
        xr = sre[pl.ds(r0, 8), :]
        xi = sim[pl.ds(r0, 8), :]
        for s, pr, pi in steps:
            sr = pltpu.roll(xr, s, 0)
            si = pltpu.roll(xi, s, 0)
            xr, xi = xr + (pr * sr - pi * si), xi + (pr * si + pi * sr)
        xr, xi = xr + pwr * cr - pwi * ci, xi + pwr * ci + pwi * cr
        sre[pl.ds(r0, 8), :] = xr
        sim[pl.ds(r0, 8), :] = xi
        return xr[7:8, :], xi[7:8, :]

    cr, ci = lax.fori_loop(0, q // 8, tile_body, (car_re[0:1, :], car_im[0:1, :]))
    car_re[0:1, :] = cr
    car_im[0:1, :] = ci

    @pl.when(c == n_chunks - 1)
    def _():
        r_last = l_real - 1 - q * (n_chunks - 1)
        sre_out[...] = sre[r_last:r_last + 1, :]
        sim_out[...] = sim[r_last:r_last + 1, :]

    y_ref[...] = _s5_tail(u, sre[...], sim[...], cdre_ref, cdim_ref, d_ref, wglu_ref, bglu_ref)


def _s5_step_kernel(u_ref, s0re_ref, s0im_ref, bdre_ref, bdim_ref, pre_ref, pim_ref, cdre_ref, cdim_ref, d_ref,
                    wglu_ref, bglu_ref, y_ref, sre_out, sim_out):
    u = u_ref[...]
    ub = u.astype(BF16)
    lr, li = pre_ref[0:1, :], pim_ref[0:1, :]
    s0r, s0i = s0re_ref[...], s0im_ref[...]
    bu_re = jnp.concatenate([_s5_in(ub, bdre_ref, k) for k in range(S5_SLABS)], axis=1)
    bu_im = jnp.concatenate([_s5_in(ub, bdim_ref, k) for k in range(S5_SLABS)], axis=1)
    s_re = bu_re + lr * s0r - li * s0i
    s_im = bu_im + lr * s0i + li * s0r
    sre_out[...] = s_re
    sim_out[...] = s_im
    y_ref[...] = _s5_tail(u, s_re, s_im, cdre_ref, cdim_ref, d_ref, wglu_ref, bglu_ref)


def _s5_const_specs(nd):
    z = (0,) * 2
    full = lambda shape: pl.BlockSpec(shape, (lambda *_: z))
    return [full((S5_W, S5_S)), full((S5_W, S5_S)), full((8, S5_S)), full((8, S5_S)),
            full((S5_S, S5_W)), full((S5_S, S5_W)), full((1, S5_W)), full((S5_W, S5_W)), full((1, S5_W))]


S5_CHUNK = 384


def _s5_prompt(proj, consts, *, bp, lp, l_real):
    nc = lp // S5_CHUNK
    kern = functools.partial(_s5_scan_kernel, l_real=l_real, n_chunks=nc)
    return pl.pallas_call(
        kern,
        grid=(bp, nc),
        in_specs=[pl.BlockSpec((S5_CHUNK, 512), lambda b, c: (b * nc + c, COL_U // 512))] + _s5_const_specs(2),
        out_specs=[pl.BlockSpec((S5_CHUNK, S5_W), lambda b, c: (b * nc + c, 0)),
                   pl.BlockSpec((None, 1, S5_S), lambda b, c: (b, 0, 0)),
                   pl.BlockSpec((None, 1, S5_S), lambda b, c: (b, 0, 0))],
        out_shape=[jax.ShapeDtypeStruct((bp * lp, S5_W), BF16),
                   jax.ShapeDtypeStruct((bp, 1, S5_S), F32),
                   jax.ShapeDtypeStruct((bp, 1, S5_S), F32)],
        scratch_shapes=[pltpu.VMEM((S5_CHUNK, S5_S), F32), pltpu.VMEM((S5_CHUNK, S5_S), F32),
                        pltpu.VMEM((8, S5_S), F32), pltpu.VMEM((8, S5_S), F32)],
        compiler_params=_cparams(("parallel", "arbitrary")),
        name="s5_prompt",
    )(proj, *consts)


def _s5_sample(proj, s0_re, s0_im, consts, *, row0, bs):
    rb = row0 // bs
    return pl.pallas_call(
        _s5_step_kernel,
        grid=(1,),
        in_specs=[pl.BlockSpec((bs, 512), lambda i: (rb, COL_U // 512)),
                  pl.BlockSpec((bs, S5_S), lambda i: (0, 0)),
                  pl.BlockSpec((bs, S5_S), lambda i: (0, 0))] + _s5_const_specs(1),
        out_specs=[pl.BlockSpec((bs, S5_W), lambda i: (0, 0)),
                   pl.BlockSpec((bs, S5_S), lambda i: (0, 0)),
                   pl.BlockSpec((bs, S5_S), lambda i: (0, 0))],
        out_shape=[jax.ShapeDtypeStruct((bs, S5_W), BF16),
                   jax.ShapeDtypeStruct((bs, S5_S), F32),
                   jax.ShapeDtypeStruct((bs, S5_S), F32)],
        compiler_params=_cparams(("arbitrary",)),
        name="s5_sample",
    )(proj, s0_re, s0_im, *consts)


ATT_TQ = 384
ATT_HG = 2
LOG2E = 1.4426950408889634


def _fox_prompt_kernel(q_ref, k_ref, v_ref, c_ref, o_ref, kb, vb):
    qi = pl.program_id(2)
    t = ATT_TQ

    @pl.when(qi == 0)
    def _():
        kb[...] = k_ref[...].astype(BF16)
        vb[...] = v_ref[...].astype(BF16)

    k2 = (ATT_D ** -0.5) * LOG2E
    qs = [(q_ref[:, h * ATT_D:(h + 1) * ATT_D] * k2).astype(BF16) for h in range(ATT_HG)]

    def block(j, carry, masked):
        start = pl.multiple_of(j * t, 128)
        out = []
        for h in range(ATT_HG):
            m, l, acc = carry[3 * h:3 * h + 3]
            hs = slice(h * ATT_D, (h + 1) * ATT_D)
            s = lax.dot_general(qs[h], kb[pl.ds(start, t), hs], (((1,), (1,)), ((), ())),
                                preferred_element_type=F32)
            s = s - c_ref[h:h + 1, pl.ds(start, t)] * LOG2E
            if masked:
                row = lax.broadcasted_iota(jnp.int32, (t, t), 0)
                col = lax.broadcasted_iota(jnp.int32, (t, t), 1)
                s = jnp.where(row >= col, s, -jnp.inf)
            m_new = jnp.maximum(m, jnp.max(s, axis=-1, keepdims=True))
            alpha = jnp.exp2(m - m_new)
            p = jnp.exp2(s - m_new)
            l = alpha * l + jnp.sum(p, axis=-1, keepdims=True)
            acc = alpha * acc + jnp.dot(p.astype(BF16), vb[pl.ds(start, t), hs], preferred_element_type=F32)
            out += [m_new, l, acc]
        return tuple(out)

    init = (jnp.full((t, 1), -jnp.inf, F32), jnp.zeros((t, 1), F32), jnp.zeros((t, ATT_D), F32)) * ATT_HG
    carry = lax.fori_loop(0, qi, lambda j, cr: block(j, cr, False), init)
    res = block(qi, carry, True)
    for h in range(ATT_HG):
        o_ref[:, h * ATT_D:(h + 1) * ATT_D] = (res[3 * h + 2] / res[3 * h + 1]).astype(BF16)


def _fox_prompt(proj, catt, *, bp, lp):
    nq = lp // ATT_TQ
    w = ATT_HG * ATT_D
    return pl.pallas_call(
        _fox_prompt_kernel,
        grid=(bp, ATT_H // ATT_HG, nq),
        in_specs=[pl.BlockSpec((ATT_TQ, w), lambda b, h, i: (b * nq + i, COL_Q // w + h)),
                  pl.BlockSpec((lp, w), lambda b, h, i: (b, COL_K // w + h)),
                  pl.BlockSpec((lp, w), lambda b, h, i: (b, COL_V // w + h)),
                  pl.BlockSpec((None, None, ATT_HG, lp), lambda b, h, i: (b, h, 0, 0))],
        out_specs=pl.BlockSpec((ATT_TQ, w), lambda b, h, i: (b * nq + i, h)),
        out_shape=jax.ShapeDtypeStruct((bp * lp, ATT_W), BF16),
        scratch_shapes=[pltpu.VMEM((lp, w), BF16), pltpu.VMEM((lp, w), BF16)],
        compiler_params=_cparams(("parallel", "parallel", "arbitrary")),
        name="fox_prompt",
    )(proj, proj, proj, catt.reshape(bp, ATT_H // ATT_HG, ATT_HG, lp))


def _fox_sample_kernel(pt_ref, q_ref, k_ref, v_ref, lfn_ref, ck_hbm, cv_hbm, cl_hbm, o_ref,
                       kbuf, vbuf, lbuf, lg, sems, *, n_seq, n_pages, page0):
    b = pl.program_id(0)
    slot = b % 2
    flat = PAGE * ATT_H

    def copies(seq, sl):
        out = []
        for j in range(n_pages):
            page = pt_ref[seq, j] + page0
            out.append(pltpu.make_async_copy(ck_hbm.at[page], kbuf.at[sl, j], sems.at[sl, 0]))
            out.append(pltpu.make_async_copy(cv_hbm.at[page], vbuf.at[sl, j], sems.at[sl, 1]))
            out.append(pltpu.make_async_copy(cl_hbm.at[pl.ds(page, 1)], lbuf.at[sl, pl.ds(j, 1)], sems.at[sl, 2]))
        return out

    @pl.when(b == 0)
    def _():
        for cp in copies(b, slot):
            cp.start()

    @pl.when(b + 1 < n_seq)
    def _():
        for cp in copies(b + 1, 1 - slot):
            cp.start()

    for cp in copies(b, slot):
        cp.wait()

    q = q_ref[...]
    qb = q.astype(BF16)
    scale = ATT_D ** -0.5
    lf = lbuf[slot]
    lane = lax.broadcasted_iota(jnp.int32, (n_pages, flat), 1)
    incl = lf
    tot = lf
    sh = ATT_H
    while sh < flat:
        incl = incl + jnp.where(lane + sh < flat, pltpu.roll(incl, flat - sh, 1), 0.0)
        tot = tot + pltpu.roll(tot, sh, 1)
        sh *= 2
    pj = lax.broadcasted_iota(jnp.int32, (n_pages, n_pages), 0)
    pk = lax.broadcasted_iota(jnp.int32, (n_pages, n_pages), 1)
    later = jnp.dot((pk > pj).astype(F32), tot, precision=HIGHEST, preferred_element_type=F32)
    bias = incl - lf + later

    hrow = lax.broadcasted_iota(jnp.int32, (ATT_H, flat), 0)
    hlane = lax.broadcasted_iota(jnp.int32, (ATT_H, flat), 1)
    own = (hlane % ATT_H) == hrow
    m = jnp.sum(q * k_ref[...], axis=-1, keepdims=True) * scale
    s_self = m
    for j in range(n_pages):
        s = lax.dot_general(qb, kbuf[slot, j].astype(BF16), (((1,), (1,)), ((), ())),
                            preferred_element_type=F32)
        s = jnp.where(own, s * scale + bias[j:j + 1, :] + lfn_ref[...], -jnp.inf)
        lg[j] = s
        m = jnp.maximum(m, jnp.max(s, axis=-1, keepdims=True))
    p_self = jnp.exp(s_self - m)
    l = p_self
    acc = p_self * v_ref[...]
    for j in range(n_pages):
        p = jnp.exp(lg[j] - m)
        l = l + jnp.sum(p, axis=-1, keepdims=True)
        acc = acc + jnp.dot(p.astype(BF16), vbuf[slot, j].astype(BF16), preferred_element_type=F32)
    o_ref[...] = acc / l


def _fox_sample(page_table, q3, k3, v3, lfn3, ck, cv, cl, page0):
    n_seq, n_pages = page_table.shape
    flat = PAGE * ATT_H
    kern = functools.partial(_fox_sample_kernel, n_seq=n_seq, n_pages=n_pages, page0=page0)
    seq_spec = pl.BlockSpec((None, ATT_H, ATT_D), lambda b, pt: (b, 0, 0))
    return pl.pallas_call(
        kern,
        grid_spec=pltpu.PrefetchScalarGridSpec(
            num_scalar_prefetch=1,
            grid=(n_seq,),
            in_specs=[seq_spec, seq_spec, seq_spec, pl.BlockSpec((None, ATT_H, 1), lambda b, pt: (b, 0, 0)),
                      pl.BlockSpec(memory_space=pl.ANY), pl.BlockSpec(memory_space=pl.ANY),
                      pl.BlockSpec(memory_space=pl.ANY)],
            out_specs=pl.BlockSpec((None, ATT_H, ATT_D), lambda b, pt: (b, 0, 0)),
            scratch_shapes=[pltpu.VMEM((2, n_pages, flat, ATT_D), F32),
                            pltpu.VMEM((2, n_pages, flat, ATT_D), F32),
                            pltpu.VMEM((2, n_pages, flat), F32),
                            pltpu.VMEM((n_pages, ATT_H, flat), F32),
                            pltpu.SemaphoreType.DMA((2, 3))]),
        out_shape=jax.ShapeDtypeStruct((n_seq, ATT_H, ATT_D), F32),
        compiler_params=_cparams(("arbitrary",)),
        name="fox_sample",
    )(page_table, q3, k3, v3, lfn3, ck, cv, cl)


def _outproj_kernel(h_ref, ya_ref, yb_ref, yc_ref, sa_ref, sb_ref, sc_ref, w_ref, o_ref, *, n_prompt_tiles):
    i = pl.program_id(0)

    def mix(a, b, c):
        return (jnp.dot(a, w_ref[0:SSD_W, :], preferred_element_type=F32)
                + jnp.dot(b, w_ref[SSD_W:SSD_W + S5_W, :], preferred_element_type=F32)
                + jnp.dot(c, w_ref[SSD_W + S5_W:, :], preferred_element_type=F32))

    @pl.when(i < n_prompt_tiles)
    def _():
        o_ref[...] = h_ref[...] + mix(ya_ref[...], yb_ref[...], yc_ref[...])

    @pl.when(i >= n_prompt_tiles)
    def _():
        bs = sa_ref.shape[0]
        o_ref[0:bs, :] = h_ref[0:bs, :] + mix(sa_ref[...], sb_ref[...], sc_ref[...])
        o_ref[bs:, :] = h_ref[bs:, :]


def _outproj(h, ya, yb, yc, sa, sb, sc, w, layer, tm=256):
    r, d = h.shape
    npt = ya.shape[0] // tm
    bs = sa.shape[0]
    assert ya.shape[0] % tm == 0 and r == (npt + 1) * tm and bs <= tm
    kern = functools.partial(_outproj_kernel, n_prompt_tiles=npt)
    pidx = lambda i: (jnp.minimum(i, npt - 1), 0)
    return pl.pallas_call(
        kern,
        grid=(npt + 1,),
        in_specs=[pl.BlockSpec((tm, d), lambda i: (i, 0)),
                  pl.BlockSpec((tm, SSD_W), pidx), pl.BlockSpec((tm, S5_W), pidx), pl.BlockSpec((tm, ATT_W), pidx),
                  pl.BlockSpec((bs, SSD_W), lambda i: (0, 0)), pl.BlockSpec((bs, S5_W), lambda i: (0, 0)),
                  pl.BlockSpec((bs, ATT_W), lambda i: (0, 0)),
                  pl.BlockSpec((None, d, d), lambda i: (layer, 0, 0))],
        out_specs=pl.BlockSpec((tm, d), lambda i: (i, 0)),
        out_shape=jax.ShapeDtypeStruct((r, d), F32),
        compiler_params=_cparams(("parallel",)),
        name="outproj",
    )(h, ya, yb, yc, sa, sb, sc, w)


def _block_diag(blocks):
    g, a, b = blocks.shape
    eye = jnp.eye(g, dtype=blocks.dtype)
    return (eye[:, None, :, None] * blocks[:, :, None, :]).reshape(g * a, g * b)


def kernel(x_prompt, x_sample, cache_k, cache_v, cache_logf, state_ssd, state_ssd_conv, state_s5_re, state_s5_im,
           page_table, meta_tokens, ffn1_norm, ffn1_gate, ffn1_up, ffn1_down, mix_norm, w_in, ssd_conv_w,
           ssd_conv_b, ssd_dt_bias, ssd_A_log, ssd_D, ssd_norm, s5_A_re, s5_A_im, s5_log_step, s5_B_re, s5_B_im,
           s5_C_re, s5_C_im, s5_D, s5_w_glu, s5_b_glu, att_q_norm, att_k_norm, att_f_bias, w_out, ffn2_norm,
           ffn2_gate, ffn2_up, ffn2_down):
    bp, seq, d = x_prompt.shape
    bs = x_sample.shape[0]
    depth = w_in.shape[0]
    n_phys = cache_k.shape[1]
    l_real = seq + N_META
    lp = -(-l_real // (3 * CHUNK)) * (3 * CHUNK)
    row_s = bp * lp
    r_rows = row_s + 2 * bs
    assert lp % ATT_TQ == 0 and lp % S5_CHUNK == 0 and row_s % 256 == 0 and r_rows % 512 == 0 and bs == 128

    meta = meta_tokens.astype(F32)
    pad = jnp.zeros((lp - l_real, d), F32)
    pieces = []
    for b in range(bp):
        pieces += [meta, x_prompt[b], pad]
    pieces += [x_sample[:, 0, :], jnp.zeros((bs, d), F32)]
    h = jnp.concatenate(pieces, axis=0)

    expand = (jnp.arange(128)[:, None] == (jnp.arange(SSD_W)[None, :] // SSD_P)).astype(F32)
    n_tiles = PROJ_W // PROJ_TN
    ck_all = cache_k.reshape(depth * n_phys, PAGE * ATT_H, ATT_D)
    cv_all = cache_v.reshape(depth * n_phys, PAGE * ATT_H, ATT_D)
    cl_all = cache_logf.reshape(depth * n_phys, PAGE * ATT_H)

    w1g, w1u, w1d = ffn1_gate.astype(BF16), ffn1_up.astype(BF16), ffn1_down.astype(BF16)
    w2g, w2u, w2d = ffn2_gate.astype(BF16), ffn2_up.astype(BF16), ffn2_down.astype(BF16)
    w_out_all = w_out.astype(BF16)
    z_w, xbc_w, dt_w, u_w, q_w, k_w, v_w, f_w = jnp.split(w_in, [512, 1536, 1544, 2056, 3080, 4104, 5128], axis=2)
    w_proj_all = jnp.concatenate(
        [z_w, xbc_w, u_w, q_w, k_w, v_w, dt_w, f_w,
         jnp.zeros((depth, d, PROJ_W - COL_DTF - 2 * SSD_HEADS), F32)], axis=2).astype(BF16)

    outs = {k: [] for k in ("k_p", "v_p", "lf_p", "k_s", "v_s", "lf_s", "ssd_p", "ssd_s", "conv_p", "conv_s",
                            "re_p", "re_s", "im_p", "im_s")}
    for i in range(depth):
        gains = jnp.ones((n_tiles, 1, PROJ_TN), F32)
        gq = jnp.tile(att_q_norm[i], PROJ_TN // ATT_D)
        gk = jnp.tile(att_k_norm[i], PROJ_TN // ATT_D)
        gains = gains.at[COL_Q // PROJ_TN:COL_K // PROJ_TN, 0].set(gq).at[COL_K // PROJ_TN:COL_V // PROJ_TN, 0].set(gk)
        bias = jnp.zeros((1, PROJ_TN), F32).at[0, 0:8].set(ssd_dt_bias[i]).at[0, 8:16].set(att_f_bias[i])
        cw, cb = ssd_conv_w[i], ssd_conv_b[i][None, :]
        alog_row = jnp.zeros((1, 128), F32).at[0, 0:SSD_HEADS].set(ssd_A_log[i])
        aexp = jnp.repeat(ssd_A_log[i], SSD_P)[None, :]
        dexp = jnp.repeat(ssd_D[i], SSD_P)[None, :]
        gn = ssd_norm[i][None, :]

        h = _ffn(h, ffn1_norm[i][None, :], w1g, w1u, w1d, i)
        proj = _inproj(h, mix_norm[i][None, :], w_proj_all, gains, bias, i)

        y_ssd_p, catt, ssd_p, conv_p, k_p, v_p, lf_p = _ssd_prompt(
            proj, cw, cb, alog_row, dexp, gn, bp=bp, lp=lp, l_real=l_real)
        y_ssd_s, conv_s, ssd_s = _ssd_sample(proj, jnp.moveaxis(state_ssd_conv[i], 1, 0), state_ssd,
                                             cw, cb, aexp, dexp, gn, expand, row0=row_s, bs=bs, layer=i)
        outs["k_p"].append(k_p[:, :l_real].reshape(bp, l_real, ATT_H, ATT_D))
        outs["v_p"].append(v_p[:, :l_real].reshape(bp, l_real, ATT_H, ATT_D))
        outs["lf_p"].append(lf_p[:, :l_real])
        outs["ssd_s"].append(ssd_s)
        lam_re, lam_im, pw_re, pw_im, bb_re, bb_im = _s5_params(
            s5_A_re[i], s5_A_im[i], s5_log_step[i][:, None],
            jnp.swapaxes(s5_B_re[i], 1, 2), jnp.swapaxes(s5_B_im[i], 1, 2))
        consts = (_block_diag(bb_re).astype(BF16), _block_diag(bb_im).astype(BF16),
                  pw_re.reshape(8, S5_S), pw_im.reshape(8, S5_S),
                  _block_diag(jnp.swapaxes(s5_C_re[i], 1, 2)).astype(BF16),
                  _block_diag(jnp.swapaxes(s5_C_im[i], 1, 2)).astype(BF16),
                  s5_D[i].reshape(1, S5_W), s5_w_glu[i].astype(BF16), s5_b_glu[i][None, :])
        y_s5_p, re_p, im_p = _s5_prompt(proj, consts, bp=bp, lp=lp, l_real=l_real)
        y_s5_s, re_s, im_s = _s5_sample(proj, state_s5_re[i].reshape(bs, S5_S), state_s5_im[i].reshape(bs, S5_S),
                                        consts, row0=row_s, bs=bs)
        y_att_p = _fox_prompt(proj, catt, bp=bp, lp=lp)
        samp = proj[row_s:row_s + bs]
        q3 = samp[:, COL_Q:COL_K].reshape(bs, ATT_H, ATT_D)
        k3 = samp[:, COL_K:COL_V].reshape(bs, ATT_H, ATT_D)
        v3 = samp[:, COL_V:COL_DTF].reshape(bs, ATT_H, ATT_D)
        lfn3 = samp[:, COL_DTF + 8:COL_DTF + 16].reshape(bs, ATT_H, 1)
        y_att_s = _fox_sample(page_table, q3, k3, v3, lfn3,
                              ck_all, cv_all, cl_all, i * n_phys)
        h = _outproj(h, y_ssd_p, y_s5_p, y_att_p, y_ssd_s, y_s5_s,
                     y_att_s.reshape(bs, ATT_W).astype(BF16), w_out_all, i)
        h = _ffn(h, ffn2_norm[i][None, :], w2g, w2u, w2d, i)

        outs["k_s"].append(k3[:, None])
        outs["v_s"].append(v3[:, None])
        outs["lf_s"].append(samp[:, None, COL_DTF + 8:COL_DTF + 16])
        outs["ssd_p"].append(jnp.swapaxes(ssd_p, 2, 3))
        outs["conv_p"].append(conv_p)
        outs["conv_s"].append(jnp.moveaxis(conv_s, 0, 1))
        outs["re_p"].append(re_p.reshape(bp, S5_G, S5_N))
        outs["re_s"].append(re_s.reshape(bs, S5_G, S5_N))
        outs["im_p"].append(im_p.reshape(bp, S5_G, S5_N))
        outs["im_s"].append(im_s.reshape(bs, S5_G, S5_N))

    st = {k: jnp.stack(v, axis=0) for k, v in outs.items()}
    y_prompt = jnp.stack([h[b * lp + N_META:b * lp + l_real] for b in range(bp)], axis=0)
    y_sample = h[row_s:row_s + bs][:, None, :]
    return (y_prompt, y_sample, st["k_p"], st["v_p"], st["lf_p"], st["k_s"], st["v_s"], st["lf_s"],
            st["ssd_p"], st["ssd_s"], st["conv_p"], st["conv_s"], st["re_p"], st["re_s"], st["im_p"], st["im_s"])
```

```python
import functools
import math

import jax
import jax.numpy as jnp
from jax import lax
from jax.experimental import pallas as pl
from jax.experimental.pallas import tpu as pltpu

F32 = jnp.float32
BF16 = jnp.bfloat16
HIGHEST = lax.Precision.HIGHEST

NORM_EPS = 1e-6
N_META = 16
PAGE = 128
CHUNK = 128
SSD_HEADS, SSD_P, SSD_N, SSD_G = 8, 64, 128, 2
SSD_W = SSD_HEADS * SSD_P
S5_G, S5_C, S5_N = 32, 16, 64
S5_W = S5_G * S5_C
S5_S = S5_G * S5_N
ATT_H, ATT_D = 8, 128
ATT_W = ATT_H * ATT_D
PROJ_TN = 512
COL_Z, COL_X, COL_BC, COL_U, COL_Q, COL_K, COL_V, COL_DTF = 0, 512, 1024, 1536, 2048, 3072, 4096, 5120
PROJ_W = 5632
VMEM_LIMIT = 56 * 1024 * 1024


def _cparams(sem):
    return pltpu.CompilerParams(dimension_semantics=sem, vmem_limit_bytes=VMEM_LIMIT)


def _rms(x, g):
    return x * lax.rsqrt(jnp.mean(x * x, axis=-1, keepdims=True) + NORM_EPS) * g


def _silu(x):
    return x * jax.nn.sigmoid(x)


def _softplus(x):
    return jnp.maximum(x, 0.0) + jnp.log1p(jnp.exp(-jnp.abs(x)))


def _ffn_kernel(x_ref, g_ref, wg_ref, wu_ref, wd_ref, o_ref, xn_ref):
    j = pl.program_id(1)

    @pl.when(j == 0)
    def _():
        x = x_ref[...]
        xn_ref[...] = _rms(x, g_ref[...]).astype(BF16)
        o_ref[...] = x

    xn = xn_ref[...]
    gate = jnp.dot(xn, wg_ref[...], preferred_element_type=F32)
    up = jnp.dot(xn, wu_ref[...], preferred_element_type=F32)
    act = (0.5 * _silu(gate) * up).astype(BF16)
    o_ref[...] += jnp.dot(act, wd_ref[...], preferred_element_type=F32)


def _ffn(h, g, wg, wu, wd, layer, tm=512, tf=512):
    r, d = h.shape
    f = wg.shape[2]
    return pl.pallas_call(
        _ffn_kernel,
        grid=(r // tm, f // tf),
        in_specs=[pl.BlockSpec((tm, d), lambda i, j: (i, 0)),
                  pl.BlockSpec((1, d), lambda i, j: (0, 0)),
                  pl.BlockSpec((None, d, tf), lambda i, j: (layer, 0, j)),
                  pl.BlockSpec((None, d, tf), lambda i, j: (layer, 0, j)),
                  pl.BlockSpec((None, tf, d), lambda i, j: (layer, j, 0))],
        out_specs=pl.BlockSpec((tm, d), lambda i, j: (i, 0)),
        out_shape=jax.ShapeDtypeStruct((r, d), F32),
        scratch_shapes=[pltpu.VMEM((tm, d), BF16)],
        compiler_params=_cparams(("parallel", "arbitrary")),
        name="ffn",
    )(h, g, wg, wu, wd)


def _inproj_kernel(x_ref, g_ref, w_ref, gain_ref, bias_ref, o_ref, xn_ref):
    j = pl.program_id(1)

    @pl.when(j == 0)
    def _():
        xn_ref[...] = _rms(x_ref[...], g_ref[...]).astype(BF16)

    acc = jnp.dot(xn_ref[...], w_ref[...], preferred_element_type=F32)
    jq, jv, jd = COL_Q // PROJ_TN, COL_V // PROJ_TN, COL_DTF // PROJ_TN
    is_qk = jnp.logical_and(j >= jq, j < jv)

    @pl.when(jnp.logical_and(jnp.logical_not(is_qk), j != jd))
    def _():
        o_ref[...] = acc

    @pl.when(is_qk)
    def _():
        for hh in range(PROJ_TN // ATT_D):
            sl = slice(hh * ATT_D, (hh + 1) * ATT_D)
            o_ref[:, sl] = _rms(acc[:, sl], gain_ref[:, sl])

    @pl.when(j == jd)
    def _():
        a = acc[:, :128] + bias_ref[:, :128]
        lane = lax.broadcasted_iota(jnp.int32, a.shape, 1)
        is_dt = lane < SSD_HEADS
        sp = _softplus(jnp.where(is_dt, a, -a))
        o_ref[:, :128] = jnp.where(is_dt, sp, -sp)
        o_ref[:, 128:] = jnp.zeros((acc.shape[0], PROJ_TN - 128), F32)


def _inproj(h, g, w, gains, bias, layer, tm=1088):
    r, d = h.shape
    n = w.shape[2]
    tn = PROJ_TN
    return pl.pallas_call(
        _inproj_kernel,
        grid=(r // tm, n // tn),
        in_specs=[pl.BlockSpec((tm, d), lambda i, j: (i, 0)),
                  pl.BlockSpec((1, d), lambda i, j: (0, 0)),
                  pl.BlockSpec((None, d, tn), lambda i, j: (layer, 0, j)),
                  pl.BlockSpec((None, 1, tn), lambda i, j: (j, 0, 0)),
                  pl.BlockSpec((1, tn), lambda i, j: (0, 0))],
        out_specs=pl.BlockSpec((tm, tn), lambda i, j: (i, j)),
        out_shape=jax.ShapeDtypeStruct((r, n), F32),
        scratch_shapes=[pltpu.VMEM((tm, d), BF16)],
        compiler_params=_cparams(("parallel", "arbitrary")),
        name="inproj",
    )(h, g, w, gains, bias)


def _ssd_gate_norm(y, z, gn):
    yz = y * _silu(z)
    gw = SSD_W // SSD_G
    outs = []
    for g in range(SSD_G):
        blk = yz[:, g * gw:(g + 1) * gw]
        outs.append(blk * lax.rsqrt(jnp.mean(blk * blk, axis=-1, keepdims=True) + NORM_EPS)
                    * gn[:, g * gw:(g + 1) * gw])
    return outs


def _ssd_prompt_kernel(z_ref, xx_ref, bc_ref, dtf_ref, cw_ref, cb_ref, alog_ref, dexp_ref, gn_ref,
                       y_ref, catt_ref, st_ref, cst_ref, lf_ref, xbuf, hst, ccarry, ybuf, *, l_real, n_chunks):
    c = pl.program_id(1)
    q = CHUNK
    lf_ref[...] = dtf_ref[:, SSD_HEADS:2 * SSD_HEADS]

    @pl.when(c == 0)
    def _():
        xbuf[0:8, :] = jnp.zeros((8, xbuf.shape[1]), F32)
        hst[...] = jnp.zeros(hst.shape, F32)
        ccarry[...] = jnp.zeros(ccarry.shape, F32)

    xbuf[8:8 + q, 0:SSD_W] = xx_ref[...]
    xbuf[8:8 + q, SSD_W:] = bc_ref[...]
    conv = cb_ref[...] + cw_ref[0:1, :] * xbuf[5:5 + q, :]
    for k in range(1, 4):
        conv = conv + cw_ref[k:k + 1, :] * xbuf[5 + k:5 + k + q, :]
    xc = _silu(conv)

    @pl.when(c == n_chunks - 1)
    def _():
        r0 = 8 + l_real - q * (n_chunks - 1) - 3
        cst_ref[...] = xbuf[r0:r0 + 3, :]

    xbuf[0:8, :] = xbuf[q:q + 8, :]

    row = lax.broadcasted_iota(jnp.int32, (q, q), 0)
    col = lax.broadcasted_iota(jnp.int32, (q, q), 1)
    lane1 = lax.broadcasted_iota(jnp.int32, (1, q), 1)
    valid = (c * q + row) < l_real
    dtm = jnp.where(valid, dtf_ref[...], 0.0)
    a_row = -jnp.exp(alog_ref[...])
    scale_row = jnp.where(lane1 < SSD_HEADS, a_row, jnp.where(lane1 < 2 * SSD_HEADS, 1.0, 0.0))
    causal = row >= col
    cs = jnp.dot(causal.astype(F32), dtm * scale_row, precision=HIGHEST, preferred_element_type=F32)
    cs_t = cs.T
    catt_ref[...] = (cs + ccarry[0:1, :]).T[SSD_HEADS:2 * SSD_HEADS, :]
    ccarry[0:1, :] = ccarry[0:1, :] + cs[q - 1:q, :]

    hpg = SSD_HEADS // SSD_G
    for g in range(SSD_G):
        bg = xc[:, SSD_W + g * SSD_N:SSD_W + (g + 1) * SSD_N]
        cg = xc[:, SSD_W + (SSD_G + g) * SSD_N:SSD_W + (SSD_G + g + 1) * SSD_N].astype(BF16)
        bg_t = bg.T.astype(BF16)
        cb = jnp.dot(cg, bg_t, preferred_element_type=F32)
        for hh in range(hpg):
            h = g * hpg + hh
            xh = xc[:, h * SSD_P:(h + 1) * SSD_P]
            acs_col = cs[:, h:h + 1]
            seg = acs_col - cs_t[h:h + 1, :]
            decay = jnp.exp(jnp.where(causal, seg, -jnp.inf))
            xdt = xh * dtm[:, h:h + 1]
            y = jnp.dot((cb * decay).astype(BF16), xdt.astype(BF16), preferred_element_type=F32)
            h_t = hst[h]
            y = y + jnp.dot(cg, h_t.astype(BF16), preferred_element_type=F32) * jnp.exp(acs_col)
            a_last = cs[q - 1:q, h:h + 1]
            xw = (xdt * jnp.exp(a_last - acs_col)).astype(BF16)
            hst[h] = h_t * jnp.exp(a_last) + jnp.dot(bg_t, xw, preferred_element_type=F32)
            ybuf[:, h * SSD_P:(h + 1) * SSD_P] = y + dexp_ref[:, h * SSD_P:(h + 1) * SSD_P] * xh

    outs = _ssd_gate_norm(ybuf[...], z_ref[...], gn_ref[...])
    gw = SSD_W // SSD_G
    for g in range(SSD_G):
        y_ref[:, g * gw:(g + 1) * gw] = outs[g].astype(BF16)

    @pl.when(c == n_chunks - 1)
    def _():
        st_ref[...] = hst[...]


def _ssd_prompt(proj, cw, cb, alog_row, dexp, gn, *, bp, lp, l_real):
    nc = lp // CHUNK
    kern = functools.partial(_ssd_prompt_kernel, l_real=l_real, n_chunks=nc)
    rowblk = lambda b, c: b * nc + c
    return pl.pallas_call(
        kern,
        grid=(bp, nc),
        in_specs=[pl.BlockSpec((CHUNK, 512), lambda b, c: (rowblk(b, c), COL_Z // 512)),
                  pl.BlockSpec((CHUNK, 512), lambda b, c: (rowblk(b, c), COL_X // 512)),
                  pl.BlockSpec((CHUNK, 512), lambda b, c: (rowblk(b, c), COL_BC // 512)),
                  pl.BlockSpec((CHUNK, 128), lambda b, c: (rowblk(b, c), COL_DTF // 128)),
                  pl.BlockSpec((4, 1024), lambda b, c: (0, 0)),
                  pl.BlockSpec((1, 1024), lambda b, c: (0, 0)),
                  pl.BlockSpec((1, 128), lambda b, c: (0, 0)),
                  pl.BlockSpec((1, 512), lambda b, c: (0, 0)),
                  pl.BlockSpec((1, 512), lambda b, c: (0, 0))],
        out_specs=[pl.BlockSpec((CHUNK, SSD_W), lambda b, c: (rowblk(b, c), 0)),
                   pl.BlockSpec((None, SSD_HEADS, CHUNK), lambda b, c: (b, 0, c)),
                   pl.BlockSpec((None, SSD_HEADS, SSD_N, SSD_P), lambda b, c: (b, 0, 0, 0)),
                   pl.BlockSpec((None, 3, 1024), lambda b, c: (b, 0, 0)),
                   pl.BlockSpec((None, CHUNK, ATT_H), lambda b, c: (b, c, 0))],
        out_shape=[jax.ShapeDtypeStruct((bp * lp, SSD_W), BF16),
                   jax.ShapeDtypeStruct((bp, ATT_H, lp), F32),
                   jax.ShapeDtypeStruct((bp, SSD_HEADS, SSD_N, SSD_P), F32),
                   jax.ShapeDtypeStruct((bp, 3, 1024), F32),
                   jax.ShapeDtypeStruct((bp, lp, ATT_H), F32)],
        scratch_shapes=[pltpu.VMEM((CHUNK + 8, 1024), F32),
                        pltpu.VMEM((SSD_HEADS, SSD_N, SSD_P), F32),
                        pltpu.VMEM((8, 128), F32),
                        pltpu.VMEM((CHUNK, SSD_W), F32)],
        compiler_params=_cparams(("parallel", "arbitrary")),
        name="ssd_prompt",
    )(proj, proj, proj, proj, cw, cb, alog_row, dexp, gn)


SSD_SB = 16


def _ssd_sample_kernel(z_ref, xx_ref, bc_ref, dtf_ref, cst_ref, st_ref, cw_ref, cb_ref, aexp_ref, dexp_ref,
                       gn_ref, expand_ref, y_ref, cst_out, st_out, dparts, xdt_t, bm_s, cmb, y_t, *, n_steps):
    i = pl.program_id(0)
    bs = z_ref.shape[0]

    @pl.when(i == 0)
    def _():
        w = cw_ref[...]
        for half, ref in ((0, xx_ref), (1, bc_ref)):
            sl = slice(half * 512, (half + 1) * 512)
            xn = ref[...]
            conv = (cb_ref[:, sl] + w[0:1, sl] * cst_ref[0, :, sl] + w[1:2, sl] * cst_ref[1, :, sl]
                    + w[2:3, sl] * cst_ref[2, :, sl] + w[3:4, sl] * xn)
            xcv = _silu(conv)
            cst_out[0, :, sl] = cst_ref[1, :, sl]
            cst_out[1, :, sl] = cst_ref[2, :, sl]
            cst_out[2, :, sl] = xn
            if half == 0:
                dt_e = jnp.dot(dtf_ref[...], expand_ref[...], precision=HIGHEST, preferred_element_type=F32)
                dec = jnp.exp(dt_e * (-jnp.exp(aexp_ref[...]))).T
                hi = dec.astype(BF16).astype(F32)
                mid = (dec - hi).astype(BF16).astype(F32)
                dparts[0] = hi
                dparts[1] = mid
                dparts[2] = dec - hi - mid
                xdt_t[...] = (xcv * dt_e).T
                y_t[...] = (dexp_ref[...] * xcv).T
            else:
                bm_s[...] = xcv[:, 0:SSD_G * SSD_N]
                cmb[...] = xcv[:, SSD_G * SSD_N:].astype(BF16)

    lane = lax.broadcasted_iota(jnp.int32, (1, bs), 1)
    shift = (bs - i * SSD_SB) % bs
    lhs = jnp.where(lane < SSD_SB, pltpu.roll(xdt_t[...], shift, 1), 0.0)
    for k in range(3):
        lo_l = (k + 1) * SSD_SB
        piece = pltpu.roll(dparts[k], (shift + lo_l) % bs, 1)
        lhs = jnp.where(jnp.logical_and(lane >= lo_l, lane < lo_l + SSD_SB), piece, lhs)
    lhs = lhs.astype(BF16)
    rowi = lax.broadcasted_iota(jnp.int32, (bs, SSD_N), 0)
    krow = rowi & (SSD_SB - 1)
    decf = jnp.where(jnp.logical_and(rowi >= SSD_SB, rowi < 4 * SSD_SB), 1.0, 0.0)
    hpg = SSD_HEADS // SSD_G
    gr = hpg * SSD_P
    yacc = [jnp.zeros((gr, bs), F32) for _ in range(SSD_G)]
    bm_blk = bm_s[pl.ds(pl.multiple_of(i * SSD_SB, SSD_SB), SSD_SB), :]
    for bb in range(SSD_SB):
        seq = i * SSD_SB + bb
        right = jnp.where(krow == bb, decf, 0.0).astype(BF16)
        is_bb = rowi == bb
        onehot = (lane == seq).astype(F32)
        for g in range(SSD_G):
            brow = bm_blk[bb:bb + 1, g * SSD_N:(g + 1) * SSD_N]
            rhs = jnp.concatenate([jnp.where(is_bb, brow, 0.0).astype(BF16), right], axis=1)
            out = jnp.dot(lhs[g * gr:(g + 1) * gr, :], rhs, preferred_element_type=F32)
            st = st_ref[bb, g * hpg:(g + 1) * hpg].reshape(gr, SSD_N)
            hn = st * out[:, SSD_N:] + out[:, :SSD_N]
            st_out[bb, g * hpg:(g + 1) * hpg] = hn.reshape(hpg, SSD_P, SSD_N)
            res = lax.dot_general(hn.astype(BF16), cmb[:, g * SSD_N:(g + 1) * SSD_N], (((1,), (1,)), ((), ())),
                                  preferred_element_type=F32)
            yacc[g] = yacc[g] + res * onehot
    for g in range(SSD_G):
        y_t[g * gr:(g + 1) * gr, :] = y_t[g * gr:(g + 1) * gr, :] + yacc[g]

    @pl.when(i == n_steps - 1)
    def _():
        outs = _ssd_gate_norm(y_t[...].T, z_ref[...], gn_ref[...])
        gw = SSD_W // SSD_G
        for g in range(SSD_G):
            y_ref[:, g * gw:(g + 1) * gw] = outs[g].astype(BF16)


def _ssd_sample(proj, cst_t, st, cw, cb, aexp, dexp, gn, expand, *, row0, bs, layer):
    n_steps = bs // SSD_SB
    rb = row0 // bs
    kern = functools.partial(_ssd_sample_kernel, n_steps=n_steps)
    full = lambda shape: pl.BlockSpec(shape, lambda i: (0,) * len(shape))
    return pl.pallas_call(
        kern,
        grid=(n_steps,),
        in_specs=[pl.BlockSpec((bs, 512), lambda i: (rb, COL_Z // 512)),
                  pl.BlockSpec((bs, 512), lambda i: (rb, COL_X // 512)),
                  pl.BlockSpec((bs, 512), lambda i: (rb, COL_BC // 512)),
                  pl.BlockSpec((bs, 128), lambda i: (rb, COL_DTF // 128)),
                  full((3, bs, 1024)),
                  pl.BlockSpec((None, SSD_SB, SSD_HEADS, SSD_P, SSD_N), lambda i: (layer, i, 0, 0, 0)),
                  full((4, 1024)), full((1, 1024)), full((1, 512)), full((1, 512)), full((1, 512)),
                  full((128, 512))],
        out_specs=[full((bs, SSD_W)),
                   full((3, bs, 1024)),
                   pl.BlockSpec((SSD_SB, SSD_HEADS, SSD_P, SSD_N), lambda i: (i, 0, 0, 0))],
        out_shape=[jax.ShapeDtypeStruct((bs, SSD_W), BF16),
                   jax.ShapeDtypeStruct((3, bs, 1024), F32),
                   jax.ShapeDtypeStruct(st.shape[1:], F32)],
        scratch_shapes=[pltpu.VMEM((3, SSD_W, bs), F32), pltpu.VMEM((SSD_W, bs), F32),
                        pltpu.VMEM((bs, SSD_G * SSD_N), F32), pltpu.VMEM((bs, SSD_G * SSD_N), BF16),
                        pltpu.VMEM((SSD_W, bs), F32)],
        compiler_params=_cparams(("arbitrary",)),
        name="ssd_sample",
    )(proj, proj, proj, proj, cst_t, st, cw, cb, aexp, dexp, gn, expand)


def _s5_param_kernel(are_ref, aim_ref, ls_ref, bre_ref, bim_ref,
                     lre_ref, lim_ref, pre_ref, pim_ref, bbre_ref, bbim_ref):
    a_re, a_im = are_ref[...], aim_ref[...]
    step = jnp.exp(ls_ref[...])
    mag = jnp.exp(a_re * step)
    lr = mag * jnp.cos(a_im * step)
    li = mag * jnp.sin(a_im * step)
    den = a_re * a_re + a_im * a_im
    zr = ((lr - 1.0) * a_re + li * a_im) / den
    zi = (li * a_re - (lr - 1.0) * a_im) / den
    lre_ref[...] = lr
    lim_ref[...] = li
    pr, pi = lr, li
    for k in range(8):
        pre_ref[k] = pr
        pim_ref[k] = pi
        pr, pi = pr * lr - pi * li, pr * li + pi * lr
    for g in range(S5_G):
        bbre_ref[g] = zr[g:g + 1, :] * bre_ref[g] - zi[g:g + 1, :] * bim_ref[g]
        bbim_ref[g] = zr[g:g + 1, :] * bim_ref[g] + zi[g:g + 1, :] * bre_ref[g]


def _s5_params(a_re, a_im, log_step, b_re_t, b_im_t):
    gn = (S5_G, S5_N)
    return pl.pallas_call(
        _s5_param_kernel,
        out_shape=[jax.ShapeDtypeStruct(gn, F32), jax.ShapeDtypeStruct(gn, F32),
                   jax.ShapeDtypeStruct((8,) + gn, F32), jax.ShapeDtypeStruct((8,) + gn, F32),
                   jax.ShapeDtypeStruct((S5_G, S5_C, S5_N), F32), jax.ShapeDtypeStruct((S5_G, S5_C, S5_N), F32)],
        name="s5_params",
    )(a_re, a_im, log_step, b_re_t, b_im_t)


def _gelu_tanh(x):
    return 0.5 * x * (1.0 + jnp.tanh(math.sqrt(2.0 / math.pi) * (x + 0.044715 * (x * x * x))))


S5_SLABS = 4


def _s5_in(ub, bd_ref, k):
    ki, ko = S5_W // S5_SLABS, S5_S // S5_SLABS
    return jnp.dot(ub[:, k * ki:(k + 1) * ki], bd_ref[k * ki:(k + 1) * ki, k * ko:(k + 1) * ko],
                   preferred_element_type=F32)


def _s5_tail(u, s_re, s_im, cdre_ref, cdim_ref, d_ref, wglu_ref, bglu_ref):
    ki, ko = S5_S // S5_SLABS, S5_W // S5_SLABS
    sr, si = s_re.astype(BF16), s_im.astype(BF16)
    ys = jnp.concatenate(
        [jnp.dot(sr[:, k * ki:(k + 1) * ki], cdre_ref[k * ki:(k + 1) * ki, k * ko:(k + 1) * ko],
                 preferred_element_type=F32)
         - jnp.dot(si[:, k * ki:(k + 1) * ki], cdim_ref[k * ki:(k + 1) * ki, k * ko:(k + 1) * ko],
                   preferred_element_type=F32) for k in range(S5_SLABS)], axis=1) + d_ref[...] * u
    g5 = _gelu_tanh(ys)
    gate = jnp.dot(g5.astype(BF16), wglu_ref[...], preferred_element_type=F32) + bglu_ref[...]
    return (g5 * jax.nn.sigmoid(gate)).astype(BF16)


def _s5_scan_kernel(u_ref, bdre_ref, bdim_ref, pre_ref, pim_ref, cdre_ref, cdim_ref, d_ref, wglu_ref, bglu_ref,
                    y_ref, sre_out, sim_out, sre, sim, car_re, car_im, *, l_real, n_chunks):
    c = pl.program_id(1)
    q = S5_CHUNK

    @pl.when(c == 0)
    def _():
        car_re[...] = jnp.zeros(car_re.shape, F32)
        car_im[...] = jnp.zeros(car_im.shape, F32)

    u = u_ref[...]
    ub = u.astype(BF16)
    ko = S5_S // S5_SLABS
    for k in range(S5_SLABS):
        sre[:, k * ko:(k + 1) * ko] = _s5_in(ub, bdre_ref, k)
        sim[:, k * ko:(k + 1) * ko] = _s5_in(ub, bdim_ref, k)
    pwr, pwi = pre_ref[...], pim_ref[...]
    row = lax.broadcasted_iota(jnp.int32, (8, S5_S), 0)
    steps = [(s, jnp.where(row >= s, pwr[s - 1:s, :], 0.0), jnp.where(row >= s, pwi[s - 1:s, :], 0.0))
             for s in (1, 2, 4)]

    def tile_body(t, carry):
        cr, ci = carry
        r0 = pl.multiple_of(t * 8, 8)
        xr = sre[pl.ds(r0, 8), :]
        xi = sim[pl.ds(r0, 8), :]
        for s, pr, pi in steps:
            sr = pltpu.roll(xr, s, 0)
            si = pltpu.roll(xi, s, 0)
            xr, xi = xr + (pr * sr - pi * si), xi + (pr * si + pi * sr)
        xr, xi = xr + pwr * cr - pwi * ci, xi + pwr * ci + pwi * cr
        sre[pl.ds(r0, 8), :] = xr
        sim[pl.ds(r0, 8), :] = xi
        return xr[7:8, :], xi[7:8, :]

    cr, ci = lax.fori_loop(0, q // 8, tile_body, (car_re[0:1, :], car_im[0:1, :]))
    car_re[0:1, :] = cr
    car_im[0:1, :] = ci

    @pl.when(c == n_chunks - 1)
    def _():
        r_last = l_real - 1 - q * (n_chunks - 1)
        sre_out[...] = sre[r_last:r_last + 1, :]
        sim_out[...] = sim[r_last:r_last + 1, :]

    y_ref[...] = _s5_tail(u, sre[...], sim[...], cdre_ref, cdim_ref, d_ref, wglu_ref, bglu_ref)


def _s5_step_kernel(u_ref, s0re_ref, s0im_ref, bdre_ref, bdim_ref, pre_ref, pim_ref, cdre_ref, cdim_ref, d_ref,
                    wglu_ref, bglu_ref, y_ref, sre_out, sim_out):
    u = u_ref[...]
    ub = u.astype(BF16)
    lr, li = pre_ref[0:1, :], pim_ref[0:1, :]
    s0r, s0i = s0re_ref[...], s0im_ref[...]
    bu_re = jnp.concatenate([_s5_in(ub, bdre_ref, k) for k in range(S5_SLABS)], axis=1)
    bu_im = jnp.concatenate([_s5_in(ub, bdim_ref, k) for k in range(S5_SLABS)], axis=1)
    s_re = bu_re + lr * s0r - li * s0i
    s_im = bu_im + lr * s0i + li * s0r
    sre_out[...] = s_re
    sim_out[...] = s_im
    y_ref[...] = _s5_tail(u, s_re, s_im, cdre_ref, cdim_ref, d_ref, wglu_ref, bglu_ref)


def _s5_const_specs(nd):
    z = (0,) * 2
    full = lambda shape: pl.BlockSpec(shape, (lambda *_: z))
    return [full((S5_W, S5_S)), full((S5_W, S5_S)), full((8, S5_S)), full((8, S5_S)),
            full((S5_S, S5_W)), full((S5_S, S5_W)), full((1, S5_W)), full((S5_W, S5_W)), full((1, S5_W))]


S5_CHUNK = 384


def _s5_prompt(proj, consts, *, bp, lp, l_real):
    nc = lp // S5_CHUNK
    kern = functools.partial(_s5_scan_kernel, l_real=l_real, n_chunks=nc)
    return pl.pallas_call(
        kern,
        grid=(bp, nc),
        in_specs=[pl.BlockSpec((S5_CHUNK, 512), lambda b, c: (b * nc + c, COL_U // 512))] + _s5_const_specs(2),
        out_specs=[pl.BlockSpec((S5_CHUNK, S5_W), lambda b, c: (b * nc + c, 0)),
                   pl.BlockSpec((None, 1, S5_S), lambda b, c: (b, 0, 0)),
                   pl.BlockSpec((None, 1, S5_S), lambda b, c: (b, 0, 0))],
        out_shape=[jax.ShapeDtypeStruct((bp * lp, S5_W), BF16),
                   jax.ShapeDtypeStruct((bp, 1, S5_S), F32),
                   jax.ShapeDtypeStruct((bp, 1, S5_S), F32)],
        scratch_shapes=[pltpu.VMEM((S5_CHUNK, S5_S), F32), pltpu.VMEM((S5_CHUNK, S5_S), F32),
                        pltpu.VMEM((8, S5_S), F32), pltpu.VMEM((8, S5_S), F32)],
        compiler_params=_cparams(("parallel", "arbitrary")),
        name="s5_prompt",
    )(proj, *consts)


def _s5_sample(proj, s0_re, s0_im, consts, *, row0, bs):
    rb = row0 // bs
    return pl.pallas_call(
        _s5_step_kernel,
        grid=(1,),
        in_specs=[pl.BlockSpec((bs, 512), lambda i: (rb, COL_U // 512)),
                  pl.BlockSpec((bs, S5_S), lambda i: (0, 0)),
                  pl.BlockSpec((bs, S5_S), lambda i: (0, 0))] + _s5_const_specs(1),
        out_specs=[pl.BlockSpec((bs, S5_W), lambda i: (0, 0)),
                   pl.BlockSpec((bs, S5_S), lambda i: (0, 0)),
                   pl.BlockSpec((bs, S5_S), lambda i: (0, 0))],
        out_shape=[jax.ShapeDtypeStruct((bs, S5_W), BF16),
                   jax.ShapeDtypeStruct((bs, S5_S), F32),
                   jax.ShapeDtypeStruct((bs, S5_S), F32)],
        compiler_params=_cparams(("arbitrary",)),
        name="s5_sample",
    )(proj, s0_re, s0_im, *consts)


ATT_TQ = 384
ATT_HG = 2
LOG2E = 1.4426950408889634


def _fox_prompt_kernel(q_ref, k_ref, v_ref, c_ref, o_ref, kb, vb):
    qi = pl.program_id(2)
    t = ATT_TQ

    @pl.when(qi == 0)
    def _():
        kb[...] = k_ref[...].astype(BF16)
        vb[...] = v_ref[...].astype(BF16)

    k2 = (ATT_D ** -0.5) * LOG2E
    qs = [(q_ref[:, h * ATT_D:(h + 1) * ATT_D] * k2).astype(BF16) for h in range(ATT_HG)]

    def block(j, carry, masked):
        start = pl.multiple_of(j * t, 128)
        out = []
        for h in range(ATT_HG):
            m, l, acc = carry[3 * h:3 * h + 3]
            hs = slice(h * ATT_D, (h + 1) * ATT_D)
            s = lax.dot_general(qs[h], kb[pl.ds(start, t), hs], (((1,), (1,)), ((), ())),
                                preferred_element_type=F32)
            s = s - c_ref[h:h + 1, pl.ds(start, t)] * LOG2E
            if masked:
                row = lax.broadcasted_iota(jnp.int32, (t, t), 0)
                col = lax.broadcasted_iota(jnp.int32, (t, t), 1)
                s = jnp.where(row >= col, s, -jnp.inf)
            m_new = jnp.maximum(m, jnp.max(s, axis=-1, keepdims=True))
            alpha = jnp.exp2(m - m_new)
            p = jnp.exp2(s - m_new)
            l = alpha * l + jnp.sum(p, axis=-1, keepdims=True)
            acc = alpha * acc + jnp.dot(p.astype(BF16), vb[pl.ds(start, t), hs], preferred_element_type=F32)
            out += [m_new, l, acc]
        return tuple(out)

    init = (jnp.full((t, 1), -jnp.inf, F32), jnp.zeros((t, 1), F32), jnp.zeros((t, ATT_D), F32)) * ATT_HG
    carry = lax.fori_loop(0, qi, lambda j, cr: block(j, cr, False), init)
    res = block(qi, carry, True)
    for h in range(ATT_HG):
        o_ref[:, h * ATT_D:(h + 1) * ATT_D] = (res[3 * h + 2] / res[3 * h + 1]).astype(BF16)


def _fox_prompt(proj, catt, *, bp, lp):
    nq = lp // ATT_TQ
    w = ATT_HG * ATT_D
    return pl.pallas_call(
        _fox_prompt_kernel,
        grid=(bp, ATT_H // ATT_HG, nq),
        in_specs=[pl.BlockSpec((ATT_TQ, w), lambda b, h, i: (b * nq + i, COL_Q // w + h)),
                  pl.BlockSpec((lp, w), lambda b, h, i: (b, COL_K // w + h)),
                  pl.BlockSpec((lp, w), lambda b, h, i: (b, COL_V // w + h)),
                  pl.BlockSpec((None, None, ATT_HG, lp), lambda b, h, i: (b, h, 0, 0))],
        out_specs=pl.BlockSpec((ATT_TQ, w), lambda b, h, i: (b * nq + i, h)),
        out_shape=jax.ShapeDtypeStruct((bp * lp, ATT_W), BF16),
        scratch_shapes=[pltpu.VMEM((lp, w), BF16), pltpu.VMEM((lp, w), BF16)],
        compiler_params=_cparams(("parallel", "parallel", "arbitrary")),
        name="fox_prompt",
    )(proj, proj, proj, catt.reshape(bp, ATT_H // ATT_HG, ATT_HG, lp))


def _fox_sample_kernel(pt_ref, q_ref, k_ref, v_ref, lfn_ref, ck_hbm, cv_hbm, cl_hbm, o_ref,
                       kbuf, vbuf, lbuf, lg, sems, *, n_seq, n_pages, page0):
    b = pl.program_id(0)
    slot = b % 2
    flat = PAGE * ATT_H

    def copies(seq, sl):
        out = []
        for j in range(n_pages):
            page = pt_ref[seq, j] + page0
            out.append(pltpu.make_async_copy(ck_hbm.at[page], kbuf.at[sl, j], sems.at[sl, 0]))
            out.append(pltpu.make_async_copy(cv_hbm.at[page], vbuf.at[sl, j], sems.at[sl, 1]))
            out.append(pltpu.make_async_copy(cl_hbm.at[pl.ds(page, 1)], lbuf.at[sl, pl.ds(j, 1)], sems.at[sl, 2]))
        return out

    @pl.when(b == 0)
    def _():
        for cp in copies(b, slot):
            cp.start()

    @pl.when(b + 1 < n_seq)
    def _():
        for cp in copies(b + 1, 1 - slot):
            cp.start()

    for cp in copies(b, slot):
        cp.wait()

    q = q_ref[...]
    qb = q.astype(BF16)
    scale = ATT_D ** -0.5
    lf = lbuf[slot]
    lane = lax.broadcasted_iota(jnp.int32, (n_pages, flat), 1)
    incl = lf
    tot = lf
    sh = ATT_H
    while sh < flat:
        incl = incl + jnp.where(lane + sh < flat, pltpu.roll(incl, flat - sh, 1), 0.0)
        tot = tot + pltpu.roll(tot, sh, 1)
        sh *= 2
    pj = lax.broadcasted_iota(jnp.int32, (n_pages, n_pages), 0)
    pk = lax.broadcasted_iota(jnp.int32, (n_pages, n_pages), 1)
    later = jnp.dot((pk > pj).astype(F32), tot, precision=HIGHEST, preferred_element_type=F32)
    bias = incl - lf + later

    hrow = lax.broadcasted_iota(jnp.int32, (ATT_H, flat), 0)
    hlane = lax.broadcasted_iota(jnp.int32, (ATT_H, flat), 1)
    own = (hlane % ATT_H) == hrow
    m = jnp.sum(q * k_ref[...], axis=-1, keepdims=True) * scale
    s_self = m
    for j in range(n_pages):
        s = lax.dot_general(qb, kbuf[slot, j].astype(BF16), (((1,), (1,)), ((), ())),
                            preferred_element_type=F32)
        s = jnp.where(own, s * scale + bias[j:j + 1, :] + lfn_ref[...], -jnp.inf)
        lg[j] = s
        m = jnp.maximum(m, jnp.max(s, axis=-1, keepdims=True))
    p_self = jnp.exp(s_self - m)
    l = p_self
    acc = p_self * v_ref[...]
    for j in range(n_pages):
        p = jnp.exp(lg[j] - m)
        l = l + jnp.sum(p, axis=-1, keepdims=True)
        acc = acc + jnp.dot(p.astype(BF16), vbuf[slot, j].astype(BF16), preferred_element_type=F32)
    o_ref[...] = acc / l


def _fox_sample(page_table, q3, k3, v3, lfn3, ck, cv, cl, page0):
    n_seq, n_pages = page_table.shape
    flat = PAGE * ATT_H
    kern = functools.partial(_fox_sample_kernel, n_seq=n_seq, n_pages=n_pages, page0=page0)
    seq_spec = pl.BlockSpec((None, ATT_H, ATT_D), lambda b, pt: (b, 0, 0))
    return pl.pallas_call(
        kern,
        grid_spec=pltpu.PrefetchScalarGridSpec(
            num_scalar_prefetch=1,
            grid=(n_seq,),
            in_specs=[seq_spec, seq_spec, seq_spec, pl.BlockSpec((None, ATT_H, 1), lambda b, pt: (b, 0, 0)),
                      pl.BlockSpec(memory_space=pl.ANY), pl.BlockSpec(memory_space=pl.ANY),
                      pl.BlockSpec(memory_space=pl.ANY)],
            out_specs=pl.BlockSpec((None, ATT_H, ATT_D), lambda b, pt: (b, 0, 0)),
            scratch_shapes=[pltpu.VMEM((2, n_pages, flat, ATT_D), F32),
                            pltpu.VMEM((2, n_pages, flat, ATT_D), F32),
                            pltpu.VMEM((2, n_pages, flat), F32),
                            pltpu.VMEM((n_pages, ATT_H, flat), F32),
                            pltpu.SemaphoreType.DMA((2, 3))]),
        out_shape=jax.ShapeDtypeStruct((n_seq, ATT_H, ATT_D), F32),
        compiler_params=_cparams(("arbitrary",)),
        name="fox_sample",
    )(page_table, q3, k3, v3, lfn3, ck, cv, cl)


def _outproj_kernel(h_ref, ya_ref, yb_ref, yc_ref, sa_ref, sb_ref, sc_ref, w_ref, o_ref, *, n_prompt_tiles):
    i = pl.program_id(0)

    def mix(a, b, c):
        return (jnp.dot(a, w_ref[0:SSD_W, :], preferred_element_type=F32)
                + jnp.dot(b, w_ref[SSD_W:SSD_W + S5_W, :], preferred_element_type=F32)
                + jnp.dot(c, w_ref[SSD_W + S5_W:, :], preferred_element_type=F32))

    @pl.when(i < n_prompt_tiles)
    def _():
        o_ref[...] = h_ref[...] + mix(ya_ref[...], yb_ref[...], yc_ref[...])

    @pl.when(i >= n_prompt_tiles)
    def _():
        bs = sa_ref.shape[0]
        o_ref[0:bs, :] = h_ref[0:bs, :] + mix(sa_ref[...], sb_ref[...], sc_ref[...])
        o_ref[bs:, :] = h_ref[bs:, :]


def _outproj(h, ya, yb, yc, sa, sb, sc, w, layer, tm=256):
    r, d = h.shape
    npt = ya.shape[0] // tm
    bs = sa.shape[0]
    assert ya.shape[0] % tm == 0 and r == (npt + 1) * tm and bs <= tm
    kern = functools.partial(_outproj_kernel, n_prompt_tiles=npt)
    pidx = lambda i: (jnp.minimum(i, npt - 1), 0)
    return pl.pallas_call(
        kern,
        grid=(npt + 1,),
        in_specs=[pl.BlockSpec((tm, d), lambda i: (i, 0)),
                  pl.BlockSpec((tm, SSD_W), pidx), pl.BlockSpec((tm, S5_W), pidx), pl.BlockSpec((tm, ATT_W), pidx),
                  pl.BlockSpec((bs, SSD_W), lambda i: (0, 0)), pl.BlockSpec((bs, S5_W), lambda i: (0, 0)),
                  pl.BlockSpec((bs, ATT_W), lambda i: (0, 0)),
                  pl.BlockSpec((None, d, d), lambda i: (layer, 0, 0))],
        out_specs=pl.BlockSpec((tm, d), lambda i: (i, 0)),
        out_shape=jax.ShapeDtypeStruct((r, d), F32),
        compiler_params=_cparams(("parallel",)),
        name="outproj",
    )(h, ya, yb, yc, sa, sb, sc, w)


def _assemble_kernel(*refs, depth, bp, lp, l_real, seq):
    projs, h_ref = refs[:depth], refs[depth]
    k_out, v_out, y_out, sems = refs[depth + 1:]
    copies = []
    for i in range(depth):
        for b in range(bp):
            rows = pl.ds(b * lp, l_real)
            copies.append((projs[i].at[rows, pl.ds(COL_K, ATT_W)], k_out.at[i, b]))
            copies.append((projs[i].at[rows, pl.ds(COL_V, ATT_W)], v_out.at[i, b]))
    for b in range(bp):
        copies.append((h_ref.at[pl.ds(b * lp + l_real - seq, seq), :], y_out.at[b]))
    dmas = [pltpu.make_async_copy(src, dst, sems.at[n]) for n, (src, dst) in enumerate(copies)]
    for d in dmas:
        d.start()
    for d in dmas:
        d.wait()


def _assemble(projs, h, *, bp, lp, l_real, seq):
    depth = len(projs)
    d = h.shape[1]
    n_copies = 2 * depth * bp + bp
    kern = functools.partial(_assemble_kernel, depth=depth, bp=bp, lp=lp, l_real=l_real, seq=seq)
    any_spec = pl.BlockSpec(memory_space=pl.ANY)
    return pl.pallas_call(
        kern,
        in_specs=[any_spec] * (depth + 1),
        out_specs=[any_spec] * 3,
        out_shape=[jax.ShapeDtypeStruct((depth, bp, l_real, ATT_W), F32),
                   jax.ShapeDtypeStruct((depth, bp, l_real, ATT_W), F32),
                   jax.ShapeDtypeStruct((bp, seq, d), F32)],
        scratch_shapes=[pltpu.SemaphoreType.DMA((n_copies,))],
        name="assemble",
    )(*projs, h)


def _block_diag(blocks):
    g, a, b = blocks.shape
    eye = jnp.eye(g, dtype=blocks.dtype)
    return (eye[:, None, :, None] * blocks[:, :, None, :]).reshape(g * a, g * b)


def kernel(x_prompt, x_sample, cache_k, cache_v, cache_logf, state_ssd, state_ssd_conv, state_s5_re, state_s5_im,
           page_table, meta_tokens, ffn1_norm, ffn1_gate, ffn1_up, ffn1_down, mix_norm, w_in, ssd_conv_w,
           ssd_conv_b, ssd_dt_bias, ssd_A_log, ssd_D, ssd_norm, s5_A_re, s5_A_im, s5_log_step, s5_B_re, s5_B_im,
           s5_C_re, s5_C_im, s5_D, s5_w_glu, s5_b_glu, att_q_norm, att_k_norm, att_f_bias, w_out, ffn2_norm,
           ffn2_gate, ffn2_up, ffn2_down):
    bp, seq, d = x_prompt.shape
    bs = x_sample.shape[0]
    depth = w_in.shape[0]
    n_phys = cache_k.shape[1]
    l_real = seq + N_META
    lp = -(-l_real // (3 * CHUNK)) * (3 * CHUNK)
    row_s = bp * lp
    r_rows = row_s + 2 * bs
    assert lp % ATT_TQ == 0 and lp % S5_CHUNK == 0 and row_s % 256 == 0 and r_rows % 512 == 0 and bs == 128

    meta = meta_tokens.astype(F32)
    pad = jnp.zeros((lp - l_real, d), F32)
    pieces = []
    for b in range(bp):
        pieces += [meta, x_prompt[b], pad]
    pieces += [x_sample[:, 0, :], jnp.zeros((bs, d), F32)]
    h = jnp.concatenate(pieces, axis=0)

    expand = (jnp.arange(128)[:, None] == (jnp.arange(SSD_W)[None, :] // SSD_P)).astype(F32)
    n_tiles = PROJ_W // PROJ_TN
    ck_all = cache_k.reshape(depth * n_phys, PAGE * ATT_H, ATT_D)
    cv_all = cache_v.reshape(depth * n_phys, PAGE * ATT_H, ATT_D)
    cl_all = cache_logf.reshape(depth * n_phys, PAGE * ATT_H)

    w1g, w1u, w1d = ffn1_gate.astype(BF16), ffn1_up.astype(BF16), ffn1_down.astype(BF16)
    w2g, w2u, w2d = ffn2_gate.astype(BF16), ffn2_up.astype(BF16), ffn2_down.astype(BF16)
    w_out_all = w_out.astype(BF16)
    z_w, xbc_w, dt_w, u_w, q_w, k_w, v_w, f_w = jnp.split(w_in, [512, 1536, 1544, 2056, 3080, 4104, 5128], axis=2)
    w_proj_all = jnp.concatenate(
        [z_w, xbc_w, u_w, q_w, k_w, v_w, dt_w, f_w,
         jnp.zeros((depth, d, PROJ_W - COL_DTF - 2 * SSD_HEADS), F32)], axis=2).astype(BF16)

    projs = []
    outs = {k: [] for k in ("lf_p", "k_s", "v_s", "lf_s", "ssd_p", "ssd_s", "conv_p", "conv_s",
                            "re_p", "re_s", "im_p", "im_s")}
    for i in range(depth):
        gains = jnp.ones((n_tiles, 1, PROJ_TN), F32)
        gq = jnp.tile(att_q_norm[i], PROJ_TN // ATT_D)
        gk = jnp.tile(att_k_norm[i], PROJ_TN // ATT_D)
        gains = gains.at[COL_Q // PROJ_TN:COL_K // PROJ_TN, 0].set(gq).at[COL_K // PROJ_TN:COL_V // PROJ_TN, 0].set(gk)
        bias = jnp.zeros((1, PROJ_TN), F32).at[0, 0:8].set(ssd_dt_bias[i]).at[0, 8:16].set(att_f_bias[i])
        cw, cb = ssd_conv_w[i], ssd_conv_b[i][None, :]
        alog_row = jnp.zeros((1, 128), F32).at[0, 0:SSD_HEADS].set(ssd_A_log[i])
        aexp = jnp.repeat(ssd_A_log[i], SSD_P)[None, :]
        dexp = jnp.repeat(ssd_D[i], SSD_P)[None, :]
        gn = ssd_norm[i][None, :]

        h = _ffn(h, ffn1_norm[i][None, :], w1g, w1u, w1d, i)
        proj = _inproj(h, mix_norm[i][None, :], w_proj_all, gains, bias, i)

        y_ssd_p, catt, ssd_p, conv_p, lf_p = _ssd_prompt(
            proj, cw, cb, alog_row, dexp, gn, bp=bp, lp=lp, l_real=l_real)
        projs.append(proj)
        y_ssd_s, conv_s, ssd_s = _ssd_sample(proj, jnp.moveaxis(state_ssd_conv[i], 1, 0), state_ssd,
                                             cw, cb, aexp, dexp, gn, expand, row0=row_s, bs=bs, layer=i)
        outs["lf_p"].append(lf_p[:, :l_real])
        outs["ssd_s"].append(ssd_s)
        lam_re, lam_im, pw_re, pw_im, bb_re, bb_im = _s5_params(
            s5_A_re[i], s5_A_im[i], s5_log_step[i][:, None],
            jnp.swapaxes(s5_B_re[i], 1, 2), jnp.swapaxes(s5_B_im[i], 1, 2))
        consts = (_block_diag(bb_re).astype(BF16), _block_diag(bb_im).astype(BF16),
                  pw_re.reshape(8, S5_S), pw_im.reshape(8, S5_S),
                  _block_diag(jnp.swapaxes(s5_C_re[i], 1, 2)).astype(BF16),
                  _block_diag(jnp.swapaxes(s5_C_im[i], 1, 2)).astype(BF16),
                  s5_D[i].reshape(1, S5_W), s5_w_glu[i].astype(BF16), s5_b_glu[i][None, :])
        y_s5_p, re_p, im_p = _s5_prompt(proj, consts, bp=bp, lp=lp, l_real=l_real)
        y_s5_s, re_s, im_s = _s5_sample(proj, state_s5_re[i].reshape(bs, S5_S), state_s5_im[i].reshape(bs, S5_S),
                                        consts, row0=row_s, bs=bs)
        y_att_p = _fox_prompt(proj, catt, bp=bp, lp=lp)
        samp = proj[row_s:row_s + bs]
        q3 = samp[:, COL_Q:COL_K].reshape(bs, ATT_H, ATT_D)
        k3 = samp[:, COL_K:COL_V].reshape(bs, ATT_H, ATT_D)
        v3 = samp[:, COL_V:COL_DTF].reshape(bs, ATT_H, ATT_D)
        lfn3 = samp[:, COL_DTF + 8:COL_DTF + 16].reshape(bs, ATT_H, 1)
        y_att_s = _fox_sample(page_table, q3, k3, v3, lfn3,
                              ck_all, cv_all, cl_all, i * n_phys)
        h = _outproj(h, y_ssd_p, y_s5_p, y_att_p, y_ssd_s, y_s5_s,
                     y_att_s.reshape(bs, ATT_W).astype(BF16), w_out_all, i)
        h = _ffn(h, ffn2_norm[i][None, :], w2g, w2u, w2d, i)

        outs["k_s"].append(k3[:, None])
        outs["v_s"].append(v3[:, None])
        outs["lf_s"].append(samp[:, None, COL_DTF + 8:COL_DTF + 16])
        outs["ssd_p"].append(jnp.swapaxes(ssd_p, 2, 3))
        outs["conv_p"].append(conv_p)
        outs["conv_s"].append(jnp.moveaxis(conv_s, 0, 1))
        outs["re_p"].append(re_p.reshape(bp, S5_G, S5_N))
        outs["re_s"].append(re_s.reshape(bs, S5_G, S5_N))
        outs["im_p"].append(im_p.reshape(bp, S5_G, S5_N))
        outs["im_s"].append(im_s.reshape(bs, S5_G, S5_N))

    st = {k: jnp.stack(v, axis=0) for k, v in outs.items()}
    k_p, v_p, y_prompt = _assemble(projs, h, bp=bp, lp=lp, l_real=l_real, seq=seq)
    shp = (depth, bp, l_real, ATT_H, ATT_D)
    y_sample = h[row_s:row_s + bs][:, None, :]
    return (y_prompt, y_sample, k_p.reshape(shp), v_p.reshape(shp), st["lf_p"], st["k_s"], st["v_s"], st["lf_s"],
            st["ssd_p"], st["ssd_s"], st["conv_p"], st["conv_s"], st["re_p"], st["re_s"], st["im_p"], st["im_s"])
```

```python
import functools
import math

import jax
import jax.numpy as jnp
from jax import lax
from jax.experimental import pallas as pl
from jax.experimental.pallas import tpu as pltpu

F32 = jnp.float32
BF16 = jnp.bfloat16
HIGHEST = lax.Precision.HIGHEST

NORM_EPS = 1e-6
N_META = 16
PAGE = 128
CHUNK = 128
SSD_HEADS, SSD_P, SSD_N, SSD_G = 8, 64, 128, 2
SSD_W = SSD_HEADS * SSD_P
S5_G, S5_C, S5_N = 32, 16, 64
S5_W = S5_G * S5_C
S5_S = S5_G * S5_N
ATT_H, ATT_D = 8, 128
ATT_W = ATT_H * ATT_D
PROJ_TN = 512
COL_Z, COL_X, COL_BC, COL_U, COL_Q, COL_K, COL_V, COL_DTF = 0, 512, 1024, 1536, 2048, 3072, 4096, 5120
PROJ_W = 5632
VMEM_LIMIT = 56 * 1024 * 1024


def _cparams(sem):
    return pltpu.CompilerParams(dimension_semantics=sem, vmem_limit_bytes=VMEM_LIMIT)


def _rms(x, g):
    return x * lax.rsqrt(jnp.mean(x * x, axis=-1, keepdims=True) + NORM_EPS) * g


def _silu(x):
    return x * jax.nn.sigmoid(x)


def _softplus(x):
    return jnp.maximum(x, 0.0) + jnp.log1p(jnp.exp(-jnp.abs(x)))


def _ffn_kernel(x_ref, g_ref, wg_ref, wu_ref, wd_ref, o_ref, xn_ref):
    j = pl.program_id(1)

    @pl.when(j == 0)
    def _():
        x = x_ref[...]
        xn_ref[...] = _rms(x, g_ref[...]).astype(BF16)
        o_ref[...] = x

    xn = xn_ref[...]
    gate = jnp.dot(xn, wg_ref[...], preferred_element_type=F32)
    up = jnp.dot(xn, wu_ref[...], preferred_element_type=F32)
    act = (0.5 * _silu(gate) * up).astype(BF16)
    o_ref[...] += jnp.dot(act, wd_ref[...], preferred_element_type=F32)


def _ffn(h, g, wg, wu, wd, layer, tm=512, tf=512):
    r, d = h.shape
    f = wg.shape[2]
    return pl.pallas_call(
        _ffn_kernel,
        grid=(r // tm, f // tf),
        in_specs=[pl.BlockSpec((tm, d), lambda i, j: (i, 0)),
                  pl.BlockSpec((1, d), lambda i, j: (0, 0)),
                  pl.BlockSpec((None, d, tf), lambda i, j: (layer, 0, j)),
                  pl.BlockSpec((None, d, tf), lambda i, j: (layer, 0, j)),
                  pl.BlockSpec((None, tf, d), lambda i, j: (layer, j, 0))],
        out_specs=pl.BlockSpec((tm, d), lambda i, j: (i, 0)),
        out_shape=jax.ShapeDtypeStruct((r, d), F32),
        scratch_shapes=[pltpu.VMEM((tm, d), BF16)],
        compiler_params=_cparams(("parallel", "arbitrary")),
        name="ffn",
    )(h, g, wg, wu, wd)


def _inproj_kernel(x_ref, g_ref, w_ref, gain_ref, bias_ref, o_ref, xn_ref):
    j = pl.program_id(1)

    @pl.when(j == 0)
    def _():
        xn_ref[...] = _rms(x_ref[...], g_ref[...]).astype(BF16)

    acc = jnp.dot(xn_ref[...], w_ref[...], preferred_element_type=F32)
    jq, jv, jd = COL_Q // PROJ_TN, COL_V // PROJ_TN, COL_DTF // PROJ_TN
    is_qk = jnp.logical_and(j >= jq, j < jv)

    @pl.when(jnp.logical_and(jnp.logical_not(is_qk), j != jd))
    def _():
        o_ref[...] = acc

    @pl.when(is_qk)
    def _():
        for hh in range(PROJ_TN // ATT_D):
            sl = slice(hh * ATT_D, (hh + 1) * ATT_D)
            o_ref[:, sl] = _rms(acc[:, sl], gain_ref[:, sl])

    @pl.when(j == jd)
    def _():
        a = acc[:, :128] + bias_ref[:, :128]
        lane = lax.broadcasted_iota(jnp.int32, a.shape, 1)
        is_dt = lane < SSD_HEADS
        sp = _softplus(jnp.where(is_dt, a, -a))
        o_ref[:, :128] = jnp.where(is_dt, sp, -sp)
        o_ref[:, 128:] = jnp.zeros((acc.shape[0], PROJ_TN - 128), F32)


def _inproj(h, g, w, gains, bias, layer, tm=1088):
    r, d = h.shape
    n = w.shape[2]
    tn = PROJ_TN
    return pl.pallas_call(
        _inproj_kernel,
        grid=(r // tm, n // tn),
        in_specs=[pl.BlockSpec((tm, d), lambda i, j: (i, 0)),
                  pl.BlockSpec((1, d), lambda i, j: (0, 0)),
                  pl.BlockSpec((None, d, tn), lambda i, j: (layer, 0, j)),
                  pl.BlockSpec((None, 1, tn), lambda i, j: (j, 0, 0)),
                  pl.BlockSpec((1, tn), lambda i, j: (0, 0))],
        out_specs=pl.BlockSpec((tm, tn), lambda i, j: (i, j)),
        out_shape=jax.ShapeDtypeStruct((r, n), F32),
        scratch_shapes=[pltpu.VMEM((tm, d), BF16)],
        compiler_params=_cparams(("parallel", "arbitrary")),
        name="inproj",
    )(h, g, w, gains, bias)


def _ssd_gate_norm(y, z, gn):
    yz = y * _silu(z)
    gw = SSD_W // SSD_G
    outs = []
    for g in range(SSD_G):
        blk = yz[:, g * gw:(g + 1) * gw]
        outs.append(blk * lax.rsqrt(jnp.mean(blk * blk, axis=-1, keepdims=True) + NORM_EPS)
                    * gn[:, g * gw:(g + 1) * gw])
    return outs


def _ssd_prompt_kernel(z_ref, xx_ref, bc_ref, dtf_ref, k_ref, v_ref, cw_ref, cb_ref, alog_ref, dexp_ref, gn_ref,
                       *rest, l_real, n_chunks, has_prev):
    if has_prev:
        pk_ref, pv_ref = rest[:2]
        rest = rest[2:]
    y_ref, catt_ref, st_ref, cst_ref, kst_ref, vst_ref, lf_ref, xbuf, hst, ccarry, ybuf = rest
    c = pl.program_id(1)
    q = CHUNK
    nl = kst_ref.shape[0]
    kst_ref[nl - 1] = k_ref[...]
    vst_ref[nl - 1] = v_ref[...]
    if has_prev:
        kst_ref[0:nl - 1] = pk_ref[...]
        vst_ref[0:nl - 1] = pv_ref[...]
    lf_ref[...] = dtf_ref[:, SSD_HEADS:2 * SSD_HEADS]

    @pl.when(c == 0)
    def _():
        xbuf[0:8, :] = jnp.zeros((8, xbuf.shape[1]), F32)
        hst[...] = jnp.zeros(hst.shape, F32)
        ccarry[...] = jnp.zeros(ccarry.shape, F32)

    xbuf[8:8 + q, 0:SSD_W] = xx_ref[...]
    xbuf[8:8 + q, SSD_W:] = bc_ref[...]
    conv = cb_ref[...] + cw_ref[0:1, :] * xbuf[5:5 + q, :]
    for k in range(1, 4):
        conv = conv + cw_ref[k:k + 1, :] * xbuf[5 + k:5 + k + q, :]
    xc = _silu(conv)

    @pl.when(c == n_chunks - 1)
    def _():
        r0 = 8 + l_real - q * (n_chunks - 1) - 3
        cst_ref[...] = xbuf[r0:r0 + 3, :]

    xbuf[0:8, :] = xbuf[q:q + 8, :]

    row = lax.broadcasted_iota(jnp.int32, (q, q), 0)
    col = lax.broadcasted_iota(jnp.int32, (q, q), 1)
    lane1 = lax.broadcasted_iota(jnp.int32, (1, q), 1)
    valid = (c * q + row) < l_real
    dtm = jnp.where(valid, dtf_ref[...], 0.0)
    a_row = -jnp.exp(alog_ref[...])
    scale_row = jnp.where(lane1 < SSD_HEADS, a_row, jnp.where(lane1 < 2 * SSD_HEADS, 1.0, 0.0))
    causal = row >= col
    cs = jnp.dot(causal.astype(F32), dtm * scale_row, precision=HIGHEST, preferred_element_type=F32)
    cs_t = cs.T
    catt_ref[...] = (cs + ccarry[0:1, :]).T[SSD_HEADS:2 * SSD_HEADS, :]
    ccarry[0:1, :] = ccarry[0:1, :] + cs[q - 1:q, :]

    hpg = SSD_HEADS // SSD_G
    for g in range(SSD_G):
        bg = xc[:, SSD_W + g * SSD_N:SSD_W + (g + 1) * SSD_N]
        cg = xc[:, SSD_W + (SSD_G + g) * SSD_N:SSD_W + (SSD_G + g + 1) * SSD_N].astype(BF16)
        bg_t = bg.T.astype(BF16)
        cb = jnp.dot(cg, bg_t, preferred_element_type=F32)
        for hh in range(hpg):
            h = g * hpg + hh
            xh = xc[:, h * SSD_P:(h + 1) * SSD_P]
            acs_col = cs[:, h:h + 1]
            seg = acs_col - cs_t[h:h + 1, :]
            decay = jnp.exp(jnp.where(causal, seg, -jnp.inf))
            xdt = xh * dtm[:, h:h + 1]
            y = jnp.dot((cb * decay).astype(BF16), xdt.astype(BF16), preferred_element_type=F32)
            h_t = hst[h]
            y = y + jnp.dot(cg, h_t.astype(BF16), preferred_element_type=F32) * jnp.exp(acs_col)
            a_last = cs[q - 1:q, h:h + 1]
            xw = (xdt * jnp.exp(a_last - acs_col)).astype(BF16)
            hst[h] = h_t * jnp.exp(a_last) + jnp.dot(bg_t, xw, preferred_element_type=F32)
            ybuf[:, h * SSD_P:(h + 1) * SSD_P] = y + dexp_ref[:, h * SSD_P:(h + 1) * SSD_P] * xh

    outs = _ssd_gate_norm(ybuf[...], z_ref[...], gn_ref[...])
    gw = SSD_W // SSD_G
    for g in range(SSD_G):
        y_ref[:, g * gw:(g + 1) * gw] = outs[g].astype(BF16)

    @pl.when(c == n_chunks - 1)
    def _():
        st_ref[...] = hst[...]


def _ssd_prompt(proj, cw, cb, alog_row, dexp, gn, prev_kv, *, bp, lp, l_real):
    nc = lp // CHUNK
    n_prev = 0 if prev_kv is None else prev_kv[0].shape[0]
    nl = n_prev + 1
    kern = functools.partial(_ssd_prompt_kernel, l_real=l_real, n_chunks=nc, has_prev=n_prev > 0)
    rowblk = lambda b, c: b * nc + c
    stack_spec = lambda n: pl.BlockSpec((n, None, CHUNK, ATT_W), lambda b, c: (0, b, c, 0))
    prev_specs = [] if n_prev == 0 else [stack_spec(n_prev)] * 2
    prev_args = [] if n_prev == 0 else list(prev_kv)
    return pl.pallas_call(
        kern,
        grid=(bp, nc),
        in_specs=[pl.BlockSpec((CHUNK, 512), lambda b, c: (rowblk(b, c), COL_Z // 512)),
                  pl.BlockSpec((CHUNK, 512), lambda b, c: (rowblk(b, c), COL_X // 512)),
                  pl.BlockSpec((CHUNK, 512), lambda b, c: (rowblk(b, c), COL_BC // 512)),
                  pl.BlockSpec((CHUNK, 128), lambda b, c: (rowblk(b, c), COL_DTF // 128)),
                  pl.BlockSpec((CHUNK, ATT_W), lambda b, c: (rowblk(b, c), COL_K // ATT_W)),
                  pl.BlockSpec((CHUNK, ATT_W), lambda b, c: (rowblk(b, c), COL_V // ATT_W)),
                  pl.BlockSpec((4, 1024), lambda b, c: (0, 0)),
                  pl.BlockSpec((1, 1024), lambda b, c: (0, 0)),
                  pl.BlockSpec((1, 128), lambda b, c: (0, 0)),
                  pl.BlockSpec((1, 512), lambda b, c: (0, 0)),
                  pl.BlockSpec((1, 512), lambda b, c: (0, 0))] + prev_specs,
        out_specs=[pl.BlockSpec((CHUNK, SSD_W), lambda b, c: (rowblk(b, c), 0)),
                   pl.BlockSpec((None, SSD_HEADS, CHUNK), lambda b, c: (b, 0, c)),
                   pl.BlockSpec((None, SSD_HEADS, SSD_N, SSD_P), lambda b, c: (b, 0, 0, 0)),
                   pl.BlockSpec((None, 3, 1024), lambda b, c: (b, 0, 0)),
                   stack_spec(nl), stack_spec(nl),
                   pl.BlockSpec((None, CHUNK, ATT_H), lambda b, c: (b, c, 0))],
        out_shape=[jax.ShapeDtypeStruct((bp * lp, SSD_W), BF16),
                   jax.ShapeDtypeStruct((bp, ATT_H, lp), F32),
                   jax.ShapeDtypeStruct((bp, SSD_HEADS, SSD_N, SSD_P), F32),
                   jax.ShapeDtypeStruct((bp, 3, 1024), F32),
                   jax.ShapeDtypeStruct((nl, bp, lp, ATT_W), F32),
                   jax.ShapeDtypeStruct((nl, bp, lp, ATT_W), F32),
                   jax.ShapeDtypeStruct((bp, lp, ATT_H), F32)],
        scratch_shapes=[pltpu.VMEM((CHUNK + 8, 1024), F32),
                        pltpu.VMEM((SSD_HEADS, SSD_N, SSD_P), F32),
                        pltpu.VMEM((8, 128), F32),
                        pltpu.VMEM((CHUNK, SSD_W), F32)],
        compiler_params=_cparams(("parallel", "arbitrary")),
        name="ssd_prompt",
    )(proj, proj, proj, proj, proj, proj, cw, cb, alog_row, dexp, gn, *prev_args)


SSD_SB = 16


def _ssd_sample_kernel(z_ref, xx_ref, bc_ref, dtf_ref, cst_ref, st_ref, cw_ref, cb_ref, aexp_ref, dexp_ref,
                       gn_ref, expand_ref, y_ref, cst_out, st_out, dparts, xdt_t, bm_s, cmb, y_t, *, n_steps):
    i = pl.program_id(0)
    bs = z_ref.shape[0]

    @pl.when(i == 0)
    def _():
        w = cw_ref[...]
        for half, ref in ((0, xx_ref), (1, bc_ref)):
            sl = slice(half * 512, (half + 1) * 512)
            xn = ref[...]
            conv = (cb_ref[:, sl] + w[0:1, sl] * cst_ref[0, :, sl] + w[1:2, sl] * cst_ref[1, :, sl]
                    + w[2:3, sl] * cst_ref[2, :, sl] + w[3:4, sl] * xn)
            xcv = _silu(conv)
            cst_out[0, :, sl] = cst_ref[1, :, sl]
            cst_out[1, :, sl] = cst_ref[2, :, sl]
            cst_out[2, :, sl] = xn
            if half == 0:
                dt_e = jnp.dot(dtf_ref[...], expand_ref[...], precision=HIGHEST, preferred_element_type=F32)
                dec = jnp.exp(dt_e * (-jnp.exp(aexp_ref[...]))).T
                hi = dec.astype(BF16).astype(F32)
                mid = (dec - hi).astype(BF16).astype(F32)
                dparts[0] = hi
                dparts[1] = mid
                dparts[2] = dec - hi - mid
                xdt_t[...] = (xcv * dt_e).T
                y_t[...] = (dexp_ref[...] * xcv).T
            else:
                bm_s[...] = xcv[:, 0:SSD_G * SSD_N]
                cmb[...] = xcv[:, SSD_G * SSD_N:].astype(BF16)

    lane = lax.broadcasted_iota(jnp.int32, (1, bs), 1)
    shift = (bs - i * SSD_SB) % bs
    lhs = jnp.where(lane < SSD_SB, pltpu.roll(xdt_t[...], shift, 1), 0.0)
    for k in range(3):
        lo_l = (k + 1) * SSD_SB
        piece = pltpu.roll(dparts[k], (shift + lo_l) % bs, 1)
        lhs = jnp.where(jnp.logical_and(lane >= lo_l, lane < lo_l + SSD_SB), piece, lhs)
    lhs = lhs.astype(BF16)
    rowi = lax.broadcasted_iota(jnp.int32, (bs, SSD_N), 0)
    krow = rowi & (SSD_SB - 1)
    decf = jnp.where(jnp.logical_and(rowi >= SSD_SB, rowi < 4 * SSD_SB), 1.0, 0.0)
    hpg = SSD_HEADS // SSD_G
    gr = hpg * SSD_P
    yacc = [jnp.zeros((gr, bs), F32) for _ in range(SSD_G)]
    bm_blk = bm_s[pl.ds(pl.multiple_of(i * SSD_SB, SSD_SB), SSD_SB), :]
    for bb in range(SSD_SB):
        seq = i * SSD_SB + bb
        right = jnp.where(krow == bb, decf, 0.0).astype(BF16)
        is_bb = rowi == bb
        onehot = (lane == seq).astype(F32)
        for g in range(SSD_G):
            brow = bm_blk[bb:bb + 1, g * SSD_N:(g + 1) * SSD_N]
            rhs = jnp.concatenate([jnp.where(is_bb, brow, 0.0).astype(BF16), right], axis=1)
            out = jnp.dot(lhs[g * gr:(g + 1) * gr, :], rhs, preferred_element_type=F32)
            st = st_ref[bb, g * hpg:(g + 1) * hpg].reshape(gr, SSD_N)
            hn = st * out[:, SSD_N:] + out[:, :SSD_N]
            st_out[bb, g * hpg:(g + 1) * hpg] = hn.reshape(hpg, SSD_P, SSD_N)
            res = lax.dot_general(hn.astype(BF16), cmb[:, g * SSD_N:(g + 1) * SSD_N], (((1,), (1,)), ((), ())),
                                  preferred_element_type=F32)
            yacc[g] = yacc[g] + res * onehot
    for g in range(SSD_G):
        y_t[g * gr:(g + 1) * gr, :] = y_t[g * gr:(g + 1) * gr, :] + yacc[g]

    @pl.when(i == n_steps - 1)
    def _():
        outs = _ssd_gate_norm(y_t[...].T, z_ref[...], gn_ref[...])
        gw = SSD_W // SSD_G
        for g in range(SSD_G):
            y_ref[:, g * gw:(g + 1) * gw] = outs[g].astype(BF16)


def _ssd_sample(proj, cst_t, st, cw, cb, aexp, dexp, gn, expand, *, row0, bs, layer):
    n_steps = bs // SSD_SB
    rb = row0 // bs
    kern = functools.partial(_ssd_sample_kernel, n_steps=n_steps)
    full = lambda shape: pl.BlockSpec(shape, lambda i: (0,) * len(shape))
    return pl.pallas_call(
        kern,
        grid=(n_steps,),
        in_specs=[pl.BlockSpec((bs, 512), lambda i: (rb, COL_Z // 512)),
                  pl.BlockSpec((bs, 512), lambda i: (rb, COL_X // 512)),
                  pl.BlockSpec((bs, 512), lambda i: (rb, COL_BC // 512)),
                  pl.BlockSpec((bs, 128), lambda i: (rb, COL_DTF // 128)),
                  full((3, bs, 1024)),
                  pl.BlockSpec((None, SSD_SB, SSD_HEADS, SSD_P, SSD_N), lambda i: (layer, i, 0, 0, 0)),
                  full((4, 1024)), full((1, 1024)), full((1, 512)), full((1, 512)), full((1, 512)),
                  full((128, 512))],
        out_specs=[full((bs, SSD_W)),
                   full((3, bs, 1024)),
                   pl.BlockSpec((SSD_SB, SSD_HEADS, SSD_P, SSD_N), lambda i: (i, 0, 0, 0))],
        out_shape=[jax.ShapeDtypeStruct((bs, SSD_W), BF16),
                   jax.ShapeDtypeStruct((3, bs, 1024), F32),
                   jax.ShapeDtypeStruct(st.shape[1:], F32)],
        scratch_shapes=[pltpu.VMEM((3, SSD_W, bs), F32), pltpu.VMEM((SSD_W, bs), F32),
                        pltpu.VMEM((bs, SSD_G * SSD_N), F32), pltpu.VMEM((bs, SSD_G * SSD_N), BF16),
                        pltpu.VMEM((SSD_W, bs), F32)],
        compiler_params=_cparams(("arbitrary",)),
        name="ssd_sample",
    )(proj, proj, proj, proj, cst_t, st, cw, cb, aexp, dexp, gn, expand)


def _s5_param_kernel(are_ref, aim_ref, ls_ref, bre_ref, bim_ref,
                     lre_ref, lim_ref, pre_ref, pim_ref, bbre_ref, bbim_ref):
    a_re, a_im = are_ref[...], aim_ref[...]
    step = jnp.exp(ls_ref[...])
    mag = jnp.exp(a_re * step)
    lr = mag * jnp.cos(a_im * step)
    li = mag * jnp.sin(a_im * step)
    den = a_re * a_re + a_im * a_im
    zr = ((lr - 1.0) * a_re + li * a_im) / den
    zi = (li * a_re - (lr - 1.0) * a_im) / den
    lre_ref[...] = lr
    lim_ref[...] = li
    pr, pi = lr, li
    for k in range(8):
        pre_ref[k] = pr
        pim_ref[k] = pi
        pr, pi = pr * lr - pi * li, pr * li + pi * lr
    for g in range(S5_G):
        bbre_ref[g] = zr[g:g + 1, :] * bre_ref[g] - zi[g:g + 1, :] * bim_ref[g]
        bbim_ref[g] = zr[g:g + 1, :] * bim_ref[g] + zi[g:g + 1, :] * bre_ref[g]


def _s5_params(a_re, a_im, log_step, b_re_t, b_im_t):
    gn = (S5_G, S5_N)
    return pl.pallas_call(
        _s5_param_kernel,
        out_shape=[jax.ShapeDtypeStruct(gn, F32), jax.ShapeDtypeStruct(gn, F32),
                   jax.ShapeDtypeStruct((8,) + gn, F32), jax.ShapeDtypeStruct((8,) + gn, F32),
                   jax.ShapeDtypeStruct((S5_G, S5_C, S5_N), F32), jax.ShapeDtypeStruct((S5_G, S5_C, S5_N), F32)],
        name="s5_params",
    )(a_re, a_im, log_step, b_re_t, b_im_t)


def _gelu_tanh(x):
    return 0.5 * x * (1.0 + jnp.tanh(math.sqrt(2.0 / math.pi) * (x + 0.044715 * (x * x * x))))


S5_SLABS = 4


def _s5_in(ub, bd_ref, k):
    ki, ko = S5_W // S5_SLABS, S5_S // S5_SLABS
    return jnp.dot(ub[:, k * ki:(k + 1) * ki], bd_ref[k * ki:(k + 1) * ki, k * ko:(k + 1) * ko],
                   preferred_element_type=F32)


def _s5_tail(u, s_re, s_im, cdre_ref, cdim_ref, d_ref, wglu_ref, bglu_ref):
    ki, ko = S5_S // S5_SLABS, S5_W // S5_SLABS
    sr, si = s_re.astype(BF16), s_im.astype(BF16)
    ys = jnp.concatenate(
        [jnp.dot(sr[:, k * ki:(k + 1) * ki], cdre_ref[k * ki:(k + 1) * ki, k * ko:(k + 1) * ko],
                 preferred_element_type=F32)
         - jnp.dot(si[:, k * ki:(k + 1) * ki], cdim_ref[k * ki:(k + 1) * ki, k * ko:(k + 1) * ko],
                   preferred_element_type=F32) for k in range(S5_SLABS)], axis=1) + d_ref[...] * u
    g5 = _gelu_tanh(ys)
    gate = jnp.dot(g5.astype(BF16), wglu_ref[...], preferred_element_type=F32) + bglu_ref[...]
    return (g5 * jax.nn.sigmoid(gate)).astype(BF16)


def _s5_scan_kernel(u_ref, bdre_ref, bdim_ref, pre_ref, pim_ref, cdre_ref, cdim_ref, d_ref, wglu_ref, bglu_ref,
                    y_ref, sre_out, sim_out, sre, sim, car_re, car_im, *, l_real, n_chunks):
    c = pl.program_id(1)
    q = S5_CHUNK

    @pl.when(c == 0)
    def _():
        car_re[...] = jnp.zeros(car_re.shape, F32)
        car_im[...] = jnp.zeros(car_im.shape, F32)

    u = u_ref[...]
    ub = u.astype(BF16)
    ko = S5_S // S5_SLABS
    for k in range(S5_SLABS):
        sre[:, k * ko:(k + 1) * ko] = _s5_in(ub, bdre_ref, k)
        sim[:, k * ko:(k + 1) * ko] = _s5_in(ub, bdim_ref, k)
    pwr, pwi = pre_ref[...], pim_ref[...]
    row = lax.broadcasted_iota(jnp.int32, (8, S5_S), 0)
    steps = [(s, jnp.where(row >= s, pwr[s - 1:s, :], 0.0), jnp.where(row >= s, pwi[s - 1:s, :], 0.0))
             for s in (1, 2, 4)]

    def tile_body(t, carry):
        cr, ci = carry
        r0 = pl.multiple_of(t * 8, 8)
        xr = sre[pl.ds(r0, 8), :]
        xi = sim[pl.ds(r0, 8), :]
        for s, pr, pi in steps:
            sr = pltpu.roll(xr, s, 0)
            si = pltpu.roll(xi, s, 0)
            xr, xi = xr + (pr * sr - pi * si), xi + (pr * si + pi * sr)
        xr, xi = xr + pwr * cr - pwi * ci, xi + pwr * ci + pwi * cr
        sre[pl.ds(r0, 8), :] = xr
        sim[pl.ds(r0, 8), :] = xi
        return xr[7:8, :], xi[7:8, :]

    cr, ci = lax.fori_loop(0, q // 8, tile_body, (car_re[0:1, :], car_im[0:1, :]))
    car_re[0:1, :] = cr
    car_im[0:1, :] = ci

    @pl.when(c == n_chunks - 1)
    def _():
        r_last = l_real - 1 - q * (n_chunks - 1)
        sre_out[...] = sre[r_last:r_last + 1, :]
        sim_out[...] = sim[r_last:r_last + 1, :]

    y_ref[...] = _s5_tail(u, sre[...], sim[...], cdre_ref, cdim_ref, d_ref, wglu_ref, bglu_ref)


def _s5_step_kernel(u_ref, s0re_ref, s0im_ref, bdre_ref, bdim_ref, pre_ref, pim_ref, cdre_ref, cdim_ref, d_ref,
                    wglu_ref, bglu_ref, y_ref, sre_out, sim_out):
    u = u_ref[...]
    ub = u.astype(BF16)
    lr, li = pre_ref[0:1, :], pim_ref[0:1, :]
    s0r, s0i = s0re_ref[...], s0im_ref[...]
    bu_re = jnp.concatenate([_s5_in(ub, bdre_ref, k) for k in range(S5_SLABS)], axis=1)
    bu_im = jnp.concatenate([_s5_in(ub, bdim_ref, k) for k in range(S5_SLABS)], axis=1)
    s_re = bu_re + lr * s0r - li * s0i
    s_im = bu_im + lr * s0i + li * s0r
    sre_out[...] = s_re
    sim_out[...] = s_im
    y_ref[...] = _s5_tail(u, s_re, s_im, cdre_ref, cdim_ref, d_ref, wglu_ref, bglu_ref)


def _s5_const_specs(nd):
    z = (0,) * 2
    full = lambda shape: pl.BlockSpec(shape, (lambda *_: z))
    return [full((S5_W, S5_S)), full((S5_W, S5_S)), full((8, S5_S)), full((8, S5_S)),
            full((S5_S, S5_W)), full((S5_S, S5_W)), full((1, S5_W)), full((S5_W, S5_W)), full((1, S5_W))]


S5_CHUNK = 384


def _s5_prompt(proj, consts, *, bp, lp, l_real):
    nc = lp // S5_CHUNK
    kern = functools.partial(_s5_scan_kernel, l_real=l_real, n_chunks=nc)
    return pl.pallas_call(
        kern,
        grid=(bp, nc),
        in_specs=[pl.BlockSpec((S5_CHUNK, 512), lambda b, c: (b * nc + c, COL_U // 512))] + _s5_const_specs(2),
        out_specs=[pl.BlockSpec((S5_CHUNK, S5_W), lambda b, c: (b * nc + c, 0)),
                   pl.BlockSpec((None, 1, S5_S), lambda b, c: (b, 0, 0)),
                   pl.BlockSpec((None, 1, S5_S), lambda b, c: (b, 0, 0))],
        out_shape=[jax.ShapeDtypeStruct((bp * lp, S5_W), BF16),
                   jax.ShapeDtypeStruct((bp, 1, S5_S), F32),
                   jax.ShapeDtypeStruct((bp, 1, S5_S), F32)],
        scratch_shapes=[pltpu.VMEM((S5_CHUNK, S5_S), F32), pltpu.VMEM((S5_CHUNK, S5_S), F32),
                        pltpu.VMEM((8, S5_S), F32), pltpu.VMEM((8, S5_S), F32)],
        compiler_params=_cparams(("parallel", "arbitrary")),
        name="s5_prompt",
    )(proj, *consts)


def _s5_sample(proj, s0_re, s0_im, consts, *, row0, bs):
    rb = row0 // bs
    return pl.pallas_call(
        _s5_step_kernel,
        grid=(1,),
        in_specs=[pl.BlockSpec((bs, 512), lambda i: (rb, COL_U // 512)),
                  pl.BlockSpec((bs, S5_S), lambda i: (0, 0)),
                  pl.BlockSpec((bs, S5_S), lambda i: (0, 0))] + _s5_const_specs(1),
        out_specs=[pl.BlockSpec((bs, S5_W), lambda i: (0, 0)),
                   pl.BlockSpec((bs, S5_S), lambda i: (0, 0)),
                   pl.BlockSpec((bs, S5_S), lambda i: (0, 0))],
        out_shape=[jax.ShapeDtypeStruct((bs, S5_W), BF16),
                   jax.ShapeDtypeStruct((bs, S5_S), F32),
                   jax.ShapeDtypeStruct((bs, S5_S), F32)],
        compiler_params=_cparams(("arbitrary",)),
        name="s5_sample",
    )(proj, s0_re, s0_im, *consts)


ATT_TQ = 384
ATT_HG = 2
LOG2E = 1.4426950408889634


def _fox_prompt_kernel(q_ref, k_ref, v_ref, c_ref, o_ref, kb, vb):
    qi = pl.program_id(2)
    t = ATT_TQ

    @pl.when(qi == 0)
    def _():
        kb[...] = k_ref[...].astype(BF16)
        vb[...] = v_ref[...].astype(BF16)

    k2 = (ATT_D ** -0.5) * LOG2E
    qs = [(q_ref[:, h * ATT_D:(h + 1) * ATT_D] * k2).astype(BF16) for h in range(ATT_HG)]

    def block(j, carry, masked):
        start = pl.multiple_of(j * t, 128)
        out = []
        for h in range(ATT_HG):
            m, l, acc = carry[3 * h:3 * h + 3]
            hs = slice(h * ATT_D, (h + 1) * ATT_D)
            s = lax.dot_general(qs[h], kb[pl.ds(start, t), hs], (((1,), (1,)), ((), ())),
                                preferred_element_type=F32)
            s = s - c_ref[h:h + 1, pl.ds(start, t)] * LOG2E
            if masked:
                row = lax.broadcasted_iota(jnp.int32, (t, t), 0)
                col = lax.broadcasted_iota(jnp.int32, (t, t), 1)
                s = jnp.where(row >= col, s, -jnp.inf)
            m_new = jnp.maximum(m, jnp.max(s, axis=-1, keepdims=True))
            alpha = jnp.exp2(m - m_new)
            p = jnp.exp2(s - m_new)
            l = alpha * l + jnp.sum(p, axis=-1, keepdims=True)
            acc = alpha * acc + jnp.dot(p.astype(BF16), vb[pl.ds(start, t), hs], preferred_element_type=F32)
            out += [m_new, l, acc]
        return tuple(out)

    init = (jnp.full((t, 1), -jnp.inf, F32), jnp.zeros((t, 1), F32), jnp.zeros((t, ATT_D), F32)) * ATT_HG
    carry = lax.fori_loop(0, qi, lambda j, cr: block(j, cr, False), init)
    res = block(qi, carry, True)
    for h in range(ATT_HG):
        o_ref[:, h * ATT_D:(h + 1) * ATT_D] = (res[3 * h + 2] / res[3 * h + 1]).astype(BF16)


def _fox_prompt(proj, catt, *, bp, lp):
    nq = lp // ATT_TQ
    w = ATT_HG * ATT_D
    return pl.pallas_call(
        _fox_prompt_kernel,
        grid=(bp, ATT_H // ATT_HG, nq),
        in_specs=[pl.BlockSpec((ATT_TQ, w), lambda b, h, i: (b * nq + i, COL_Q // w + h)),
                  pl.BlockSpec((lp, w), lambda b, h, i: (b, COL_K // w + h)),
                  pl.BlockSpec((lp, w), lambda b, h, i: (b, COL_V // w + h)),
                  pl.BlockSpec((None, None, ATT_HG, lp), lambda b, h, i: (b, h, 0, 0))],
        out_specs=pl.BlockSpec((ATT_TQ, w), lambda b, h, i: (b * nq + i, h)),
        out_shape=jax.ShapeDtypeStruct((bp * lp, ATT_W), BF16),
        scratch_shapes=[pltpu.VMEM((lp, w), BF16), pltpu.VMEM((lp, w), BF16)],
        compiler_params=_cparams(("parallel", "parallel", "arbitrary")),
        name="fox_prompt",
    )(proj, proj, proj, catt.reshape(bp, ATT_H // ATT_HG, ATT_HG, lp))


def _fox_sample_kernel(pt_ref, q_ref, k_ref, v_ref, lfn_ref, ck_hbm, cv_hbm, cl_hbm, o_ref,
                       kbuf, vbuf, lbuf, lg, sems, *, n_seq, n_pages, page0):
    b = pl.program_id(0)
    slot = b % 2
    flat = PAGE * ATT_H

    def copies(seq, sl):
        out = []
        for j in range(n_pages):
            page = pt_ref[seq, j] + page0
            out.append(pltpu.make_async_copy(ck_hbm.at[page], kbuf.at[sl, j], sems.at[sl, 0]))
            out.append(pltpu.make_async_copy(cv_hbm.at[page], vbuf.at[sl, j], sems.at[sl, 1]))
            out.append(pltpu.make_async_copy(cl_hbm.at[pl.ds(page, 1)], lbuf.at[sl, pl.ds(j, 1)], sems.at[sl, 2]))
        return out

    @pl.when(b == 0)
    def _():
        for cp in copies(b, slot):
            cp.start()

    @pl.when(b + 1 < n_seq)
    def _():
        for cp in copies(b + 1, 1 - slot):
            cp.start()

    for cp in copies(b, slot):
        cp.wait()

    q = q_ref[...]
    qb = q.astype(BF16)
    scale = ATT_D ** -0.5
    lf = lbuf[slot]
    lane = lax.broadcasted_iota(jnp.int32, (n_pages, flat), 1)
    incl = lf
    tot = lf
    sh = ATT_H
    while sh < flat:
        incl = incl + jnp.where(lane + sh < flat, pltpu.roll(incl, flat - sh, 1), 0.0)
        tot = tot + pltpu.roll(tot, sh, 1)
        sh *= 2
    pj = lax.broadcasted_iota(jnp.int32, (n_pages, n_pages), 0)
    pk = lax.broadcasted_iota(jnp.int32, (n_pages, n_pages), 1)
    later = jnp.dot((pk > pj).astype(F32), tot, precision=HIGHEST, preferred_element_type=F32)
    bias = incl - lf + later

    hrow = lax.broadcasted_iota(jnp.int32, (ATT_H, flat), 0)
    hlane = lax.broadcasted_iota(jnp.int32, (ATT_H, flat), 1)
    own = (hlane % ATT_H) == hrow
    m = jnp.sum(q * k_ref[...], axis=-1, keepdims=True) * scale
    s_self = m
    for j in range(n_pages):
        s = lax.dot_general(qb, kbuf[slot, j].astype(BF16), (((1,), (1,)), ((), ())),
                            preferred_element_type=F32)
        s = jnp.where(own, s * scale + bias[j:j + 1, :] + lfn_ref[...], -jnp.inf)
        lg[j] = s
        m = jnp.maximum(m, jnp.max(s, axis=-1, keepdims=True))
    p_self = jnp.exp(s_self - m)
    l = p_self
    acc = p_self * v_ref[...]
    for j in range(n_pages):
        p = jnp.exp(lg[j] - m)
        l = l + jnp.sum(p, axis=-1, keepdims=True)
        acc = acc + jnp.dot(p.astype(BF16), vbuf[slot, j].astype(BF16), preferred_element_type=F32)
    o_ref[...] = acc / l


def _fox_sample(page_table, q3, k3, v3, lfn3, ck, cv, cl, page0):
    n_seq, n_pages = page_table.shape
    flat = PAGE * ATT_H
    kern = functools.partial(_fox_sample_kernel, n_seq=n_seq, n_pages=n_pages, page0=page0)
    seq_spec = pl.BlockSpec((None, ATT_H, ATT_D), lambda b, pt: (b, 0, 0))
    return pl.pallas_call(
        kern,
        grid_spec=pltpu.PrefetchScalarGridSpec(
            num_scalar_prefetch=1,
            grid=(n_seq,),
            in_specs=[seq_spec, seq_spec, seq_spec, pl.BlockSpec((None, ATT_H, 1), lambda b, pt: (b, 0, 0)),
                      pl.BlockSpec(memory_space=pl.ANY), pl.BlockSpec(memory_space=pl.ANY),
                      pl.BlockSpec(memory_space=pl.ANY)],
            out_specs=pl.BlockSpec((None, ATT_H, ATT_D), lambda b, pt: (b, 0, 0)),
            scratch_shapes=[pltpu.VMEM((2, n_pages, flat, ATT_D), F32),
                            pltpu.VMEM((2, n_pages, flat, ATT_D), F32),
                            pltpu.VMEM((2, n_pages, flat), F32),
                            pltpu.VMEM((n_pages, ATT_H, flat), F32),
                            pltpu.SemaphoreType.DMA((2, 3))]),
        out_shape=jax.ShapeDtypeStruct((n_seq, ATT_H, ATT_D), F32),
        compiler_params=_cparams(("arbitrary",)),
        name="fox_sample",
    )(page_table, q3, k3, v3, lfn3, ck, cv, cl)


def _outproj_kernel(h_ref, ya_ref, yb_ref, yc_ref, sa_ref, sb_ref, sc_ref, w_ref, o_ref, *, n_prompt_tiles):
    i = pl.program_id(0)

    def mix(a, b, c):
        return (jnp.dot(a, w_ref[0:SSD_W, :], preferred_element_type=F32)
                + jnp.dot(b, w_ref[SSD_W:SSD_W + S5_W, :], preferred_element_type=F32)
                + jnp.dot(c, w_ref[SSD_W + S5_W:, :], preferred_element_type=F32))

    @pl.when(i < n_prompt_tiles)
    def _():
        o_ref[...] = h_ref[...] + mix(ya_ref[...], yb_ref[...], yc_ref[...])

    @pl.when(i >= n_prompt_tiles)
    def _():
        bs = sa_ref.shape[0]
        o_ref[0:bs, :] = h_ref[0:bs, :] + mix(sa_ref[...], sb_ref[...], sc_ref[...])
        o_ref[bs:, :] = h_ref[bs:, :]


def _outproj(h, ya, yb, yc, sa, sb, sc, w, layer, tm=256):
    r, d = h.shape
    npt = ya.shape[0] // tm
    bs = sa.shape[0]
    assert ya.shape[0] % tm == 0 and r == (npt + 1) * tm and bs <= tm
    kern = functools.partial(_outproj_kernel, n_prompt_tiles=npt)
    pidx = lambda i: (jnp.minimum(i, npt - 1), 0)
    return pl.pallas_call(
        kern,
        grid=(npt + 1,),
        in_specs=[pl.BlockSpec((tm, d), lambda i: (i, 0)),
                  pl.BlockSpec((tm, SSD_W), pidx), pl.BlockSpec((tm, S5_W), pidx), pl.BlockSpec((tm, ATT_W), pidx),
                  pl.BlockSpec((bs, SSD_W), lambda i: (0, 0)), pl.BlockSpec((bs, S5_W), lambda i: (0, 0)),
                  pl.BlockSpec((bs, ATT_W), lambda i: (0, 0)),
                  pl.BlockSpec((None, d, d), lambda i: (layer, 0, 0))],
        out_specs=pl.BlockSpec((tm, d), lambda i: (i, 0)),
        out_shape=jax.ShapeDtypeStruct((r, d), F32),
        compiler_params=_cparams(("parallel",)),
        name="outproj",
    )(h, ya, yb, yc, sa, sb, sc, w)


def _block_diag(blocks):
    g, a, b = blocks.shape
    eye = jnp.eye(g, dtype=blocks.dtype)
    return (eye[:, None, :, None] * blocks[:, :, None, :]).reshape(g * a, g * b)


def kernel(x_prompt, x_sample, cache_k, cache_v, cache_logf, state_ssd, state_ssd_conv, state_s5_re, state_s5_im,
           page_table, meta_tokens, ffn1_norm, ffn1_gate, ffn1_up, ffn1_down, mix_norm, w_in, ssd_conv_w,
           ssd_conv_b, ssd_dt_bias, ssd_A_log, ssd_D, ssd_norm, s5_A_re, s5_A_im, s5_log_step, s5_B_re, s5_B_im,
           s5_C_re, s5_C_im, s5_D, s5_w_glu, s5_b_glu, att_q_norm, att_k_norm, att_f_bias, w_out, ffn2_norm,
           ffn2_gate, ffn2_up, ffn2_down):
    bp, seq, d = x_prompt.shape
    bs = x_sample.shape[0]
    depth = w_in.shape[0]
    n_phys = cache_k.shape[1]
    l_real = seq + N_META
    lp = -(-l_real // (3 * CHUNK)) * (3 * CHUNK)
    row_s = bp * lp
    r_rows = row_s + 2 * bs
    assert lp % ATT_TQ == 0 and lp % S5_CHUNK == 0 and row_s % 256 == 0 and r_rows % 512 == 0 and bs == 128

    meta = meta_tokens.astype(F32)
    pad = jnp.zeros((lp - l_real, d), F32)
    pieces = []
    for b in range(bp):
        pieces += [meta, x_prompt[b], pad]
    pieces += [x_sample[:, 0, :], jnp.zeros((bs, d), F32)]
    h = jnp.concatenate(pieces, axis=0)

    expand = (jnp.arange(128)[:, None] == (jnp.arange(SSD_W)[None, :] // SSD_P)).astype(F32)
    n_tiles = PROJ_W // PROJ_TN
    ck_all = cache_k.reshape(depth * n_phys, PAGE * ATT_H, ATT_D)
    cv_all = cache_v.reshape(depth * n_phys, PAGE * ATT_H, ATT_D)
    cl_all = cache_logf.reshape(depth * n_phys, PAGE * ATT_H)

    w1g, w1u, w1d = ffn1_gate.astype(BF16), ffn1_up.astype(BF16), ffn1_down.astype(BF16)
    w2g, w2u, w2d = ffn2_gate.astype(BF16), ffn2_up.astype(BF16), ffn2_down.astype(BF16)
    w_out_all = w_out.astype(BF16)
    z_w, xbc_w, dt_w, u_w, q_w, k_w, v_w, f_w = jnp.split(w_in, [512, 1536, 1544, 2056, 3080, 4104, 5128], axis=2)
    w_proj_all = jnp.concatenate(
        [z_w, xbc_w, u_w, q_w, k_w, v_w, dt_w, f_w,
         jnp.zeros((depth, d, PROJ_W - COL_DTF - 2 * SSD_HEADS), F32)], axis=2).astype(BF16)

    prev_kv = None
    outs = {k: [] for k in ("lf_p", "k_s", "v_s", "lf_s", "ssd_p", "ssd_s", "conv_p", "conv_s",
                            "re_p", "re_s", "im_p", "im_s")}
    for i in range(depth):
        gains = jnp.ones((n_tiles, 1, PROJ_TN), F32)
        gq = jnp.tile(att_q_norm[i], PROJ_TN // ATT_D)
        gk = jnp.tile(att_k_norm[i], PROJ_TN // ATT_D)
        gains = gains.at[COL_Q // PROJ_TN:COL_K // PROJ_TN, 0].set(gq).at[COL_K // PROJ_TN:COL_V // PROJ_TN, 0].set(gk)
        bias = jnp.zeros((1, PROJ_TN), F32).at[0, 0:8].set(ssd_dt_bias[i]).at[0, 8:16].set(att_f_bias[i])
        cw, cb = ssd_conv_w[i], ssd_conv_b[i][None, :]
        alog_row = jnp.zeros((1, 128), F32).at[0, 0:SSD_HEADS].set(ssd_A_log[i])
        aexp = jnp.repeat(ssd_A_log[i], SSD_P)[None, :]
        dexp = jnp.repeat(ssd_D[i], SSD_P)[None, :]
        gn = ssd_norm[i][None, :]

        h = _ffn(h, ffn1_norm[i][None, :], w1g, w1u, w1d, i)
        proj = _inproj(h, mix_norm[i][None, :], w_proj_all, gains, bias, i)

        y_ssd_p, catt, ssd_p, conv_p, k_st, v_st, lf_p = _ssd_prompt(
            proj, cw, cb, alog_row, dexp, gn, prev_kv, bp=bp, lp=lp, l_real=l_real)
        prev_kv = (k_st, v_st)
        y_ssd_s, conv_s, ssd_s = _ssd_sample(proj, jnp.moveaxis(state_ssd_conv[i], 1, 0), state_ssd,
                                             cw, cb, aexp, dexp, gn, expand, row0=row_s, bs=bs, layer=i)
        outs["lf_p"].append(lf_p[:, :l_real])
        outs["ssd_s"].append(ssd_s)
        lam_re, lam_im, pw_re, pw_im, bb_re, bb_im = _s5_params(
            s5_A_re[i], s5_A_im[i], s5_log_step[i][:, None],
            jnp.swapaxes(s5_B_re[i], 1, 2), jnp.swapaxes(s5_B_im[i], 1, 2))
        consts = (_block_diag(bb_re).astype(BF16), _block_diag(bb_im).astype(BF16),
                  pw_re.reshape(8, S5_S), pw_im.reshape(8, S5_S),
                  _block_diag(jnp.swapaxes(s5_C_re[i], 1, 2)).astype(BF16),
                  _block_diag(jnp.swapaxes(s5_C_im[i], 1, 2)).astype(BF16),
                  s5_D[i].reshape(1, S5_W), s5_w_glu[i].astype(BF16), s5_b_glu[i][None, :])
        y_s5_p, re_p, im_p = _s5_prompt(proj, consts, bp=bp, lp=lp, l_real=l_real)
        y_s5_s, re_s, im_s = _s5_sample(proj, state_s5_re[i].reshape(bs, S5_S), state_s5_im[i].reshape(bs, S5_S),
                                        consts, row0=row_s, bs=bs)
        y_att_p = _fox_prompt(proj, catt, bp=bp, lp=lp)
        samp = proj[row_s:row_s + bs]
        q3 = samp[:, COL_Q:COL_K].reshape(bs, ATT_H, ATT_D)
        k3 = samp[:, COL_K:COL_V].reshape(bs, ATT_H, ATT_D)
        v3 = samp[:, COL_V:COL_DTF].reshape(bs, ATT_H, ATT_D)
        lfn3 = samp[:, COL_DTF + 8:COL_DTF + 16].reshape(bs, ATT_H, 1)
        y_att_s = _fox_sample(page_table, q3, k3, v3, lfn3,
                              ck_all, cv_all, cl_all, i * n_phys)
        h = _outproj(h, y_ssd_p, y_s5_p, y_att_p, y_ssd_s, y_s5_s,
                     y_att_s.reshape(bs, ATT_W).astype(BF16), w_out_all, i)
        h = _ffn(h, ffn2_norm[i][None, :], w2g, w2u, w2d, i)

        outs["k_s"].append(k3[:, None])
        outs["v_s"].append(v3[:, None])
        outs["lf_s"].append(samp[:, None, COL_DTF + 8:COL_DTF + 16])
        outs["ssd_p"].append(jnp.swapaxes(ssd_p, 2, 3))
        outs["conv_p"].append(conv_p)
        outs["conv_s"].append(jnp.moveaxis(conv_s, 0, 1))
        outs["re_p"].append(re_p.reshape(bp, S5_G, S5_N))
        outs["re_s"].append(re_s.reshape(bs, S5_G, S5_N))
        outs["im_p"].append(im_p.reshape(bp, S5_G, S5_N))
        outs["im_s"].append(im_s.reshape(bs, S5_G, S5_N))

    st = {k: jnp.stack(v, axis=0) for k, v in outs.items()}
    y_prompt = jnp.stack([h[b * lp + N_META:b * lp + l_real] for b in range(bp)], axis=0)
    shp = (depth, bp, l_real, ATT_H, ATT_D)
    k_p, v_p = prev_kv[0][:, :, :l_real], prev_kv[1][:, :, :l_real]
    y_sample = h[row_s:row_s + bs][:, None, :]
    return (y_prompt, y_sample, k_p.reshape(shp), v_p.reshape(shp), st["lf_p"], st["k_s"], st["v_s"], st["lf_s"],
            st["ssd_p"], st["ssd_s"], st["conv_p"], st["conv_s"], st["re_p"], st["re_s"], st["im_p"], st["im_s"])
```
